```python
import math
import jax, jax.numpy as jnp
from jax import lax
import numpy as np

D_MODEL = 2048
BATCH = 1
SEQ = 8192
DEPTH = 2
DEC_BATCH = 2
DEC_SEQ = 16384
PAST_LEN = 128

N_EVEN = (DEPTH + 1) // 2
N_ODD = DEPTH // 2
N_NORMS = 6
D_FF = 5632
EPS = 1e-6

MLA_HEADS = 8
Q_LORA = 768
KV_LORA = 512
QK_NOPE = 128
QK_ROPE = 64
V_HEAD = 128
ROPE_THETA = 10000.0
MLA_BLOCK = 128
MLA_OUT = MLA_HEADS * V_HEAD

GM_GROUPS = 8
GM_GROUP_DIM = 128
GM_WIDTH = GM_GROUPS * GM_GROUP_DIM
GM_CHUNK = 128

IN_A_COLS = Q_LORA + KV_LORA + QK_ROPE
IN_EVEN = IN_A_COLS + 2 * GM_WIDTH
EVEN_MIX = MLA_OUT + GM_WIDTH

GQA_HEADS = 16
GQA_KV_HEADS = 4
GQA_GROUP = GQA_HEADS // GQA_KV_HEADS
HEAD_DIM = 128
WINDOW = 128
WIN_BLOCK = 128
QKV_COLS = (GQA_HEADS + 2 * GQA_KV_HEADS) * HEAD_DIM

kernel_name = "hybrid_mla_gmlp_swa_encoder"


def rmsnorm(x, g):
    xf = x.astype(jnp.float32)
    y = xf * lax.rsqrt(jnp.mean(xf * xf, axis=-1, keepdims=True) + EPS)
    return (y * g.astype(jnp.float32)).astype(x.dtype)


def layernorm(x, g):
    xf = x.astype(jnp.float32)
    mu = jnp.mean(xf, axis=-1, keepdims=True)
    xc = xf - mu
    y = xc * lax.rsqrt(jnp.mean(xc * xc, axis=-1, keepdims=True) + EPS)
    return (y * g.astype(jnp.float32)).astype(x.dtype)


def swiglu(x, w_gate, w_up, w_down):
    return (jax.nn.silu(x @ w_gate) * (x @ w_up)) @ w_down


def rope_tables(s, dtype):
    pos = jnp.arange(s, dtype=jnp.float32)
    inv = ROPE_THETA ** (-jnp.arange(0, QK_ROPE, 2, dtype=jnp.float32) / QK_ROPE)
    ang = pos[:, None] * inv[None, :]
    return jnp.cos(ang).astype(dtype), jnp.sin(ang).astype(dtype)


def apply_rope(x, cos, sin):
    half = QK_ROPE // 2
    x1, x2 = x[..., :half], x[..., half:]
    return jnp.concatenate([x1 * cos - x2 * sin, x2 * cos + x1 * sin], axis=-1)


def alibi_slopes(n):
    return jnp.asarray([2.0 ** (-8.0 * (i + 1) / n) for i in range(n)], dtype=jnp.float32)


def mla_attention(c_q, c_kv, k_r, w_uq, w_ukv):
    bsz, s, _ = c_q.shape
    q = (c_q @ w_uq).reshape(bsz, s, MLA_HEADS, QK_NOPE + QK_ROPE)
    kv = (c_kv @ w_ukv).reshape(bsz, s, MLA_HEADS, QK_NOPE + V_HEAD)
    cos, sin = rope_tables(s, c_q.dtype)
    q_rope = apply_rope(q[..., QK_NOPE:], cos[:, None, :], sin[:, None, :])
    k_rope = apply_rope(k_r, cos, sin)
    q = jnp.concatenate([q[..., :QK_NOPE], q_rope], axis=-1)
    k = jnp.concatenate([kv[..., :QK_NOPE],
                         jnp.broadcast_to(k_rope[:, :, None, :], (bsz, s, MLA_HEADS, QK_ROPE))], axis=-1)
    v = kv[..., QK_NOPE:]
    nb = s // MLA_BLOCK
    qb = q.reshape(bsz, nb, MLA_BLOCK, MLA_HEADS, QK_NOPE + QK_ROPE).transpose(1, 0, 2, 3, 4)
    scale = (QK_NOPE + QK_ROPE) ** -0.5

    def block(qi):
        logits = jnp.einsum('bqhd,bkhd->bhqk', qi, k).astype(jnp.float32) * scale
        probs = jax.nn.softmax(logits, axis=-1).astype(v.dtype)
        return jnp.einsum('bhqk,bkhd->bqhd', probs, v)

    o = lax.map(block, qb)
    return o.transpose(1, 0, 2, 3, 4).reshape(bsz, s, MLA_OUT)


def spatial_gating(u, v, g_v, w_s, b_s):
    bsz, s, _ = u.shape
    u = jax.nn.gelu(u)
    v = layernorm(jax.nn.gelu(v), g_v)
    nc = s // GM_CHUNK
    vr = v.reshape(bsz, nc, GM_CHUNK, GM_GROUPS, GM_GROUP_DIM)
    mixed = jnp.einsum('gts,bcsgd->bctgd', w_s, vr) + b_s.T[:, :, None]
    return u * mixed.reshape(bsz, s, GM_WIDTH)


def even_mixer(h, w_in, g_q, g_kv, w_uq, w_ukv, g_v, w_s, b_s, w_o):
    p = h @ w_in
    c_q = rmsnorm(p[..., :Q_LORA], g_q)
    c_kv = rmsnorm(p[..., Q_LORA:Q_LORA + KV_LORA], g_kv)
    k_r = p[..., Q_LORA + KV_LORA:IN_A_COLS]
    u = p[..., IN_A_COLS:IN_A_COLS + GM_WIDTH]
    v = p[..., IN_A_COLS + GM_WIDTH:]
    a = mla_attention(c_q, c_kv, k_r, w_uq, w_ukv)
    b = spatial_gating(u, v, g_v, w_s, b_s)
    return jnp.concatenate([a, b], axis=-1) @ w_o


def window_gqa(h, w_qkv, sink, w_o):
    bsz, s, _ = h.shape
    qkv = h @ w_qkv
    nq = GQA_HEADS * HEAD_DIM
    nk = GQA_KV_HEADS * HEAD_DIM
    q = qkv[..., :nq].reshape(bsz, s, GQA_KV_HEADS, GQA_GROUP, HEAD_DIM)
    k = qkv[..., nq:nq + nk].reshape(bsz, s, GQA_KV_HEADS, HEAD_DIM)
    v = qkv[..., nq + nk:].reshape(bsz, s, GQA_KV_HEADS, HEAD_DIM)
    nb = s // WIN_BLOCK
    pad = ((0, 0), (WIN_BLOCK, WIN_BLOCK), (0, 0), (0, 0))
    kb = jnp.pad(k, pad).reshape(bsz, nb + 2, WIN_BLOCK, GQA_KV_HEADS, HEAD_DIM)
    vb = jnp.pad(v, pad).reshape(bsz, nb + 2, WIN_BLOCK, GQA_KV_HEADS, HEAD_DIM)
    kw = jnp.concatenate([kb[:, :-2], kb[:, 1:-1], kb[:, 2:]], axis=2)
    vw = jnp.concatenate([vb[:, :-2], vb[:, 1:-1], vb[:, 2:]], axis=2)
    qb = q.reshape(bsz, nb, WIN_BLOCK, GQA_KV_HEADS, GQA_GROUP, HEAD_DIM)
    a_idx = jnp.arange(WIN_BLOCK)[:, None]
    j_idx = jnp.arange(3 * WIN_BLOCK)[None, :]
    rel = j_idx - WIN_BLOCK - a_idx
    key_pos = jnp.arange(nb)[:, None] * WIN_BLOCK - WIN_BLOCK + jnp.arange(3 * WIN_BLOCK)[None, :]
    valid = (jnp.abs(rel) <= WINDOW)[None] & ((key_pos >= 0) & (key_pos < s))[:, None, :]
    dist = jnp.abs(rel).astype(jnp.float32)
    slopes = alibi_slopes(GQA_HEADS).reshape(GQA_KV_HEADS, GQA_GROUP)
    scale = HEAD_DIM ** -0.5
    logits = (jnp.einsum('bnqkgd,bnjkd->bnkgqj', qb, kw).astype(jnp.float32) * scale
              - slopes[:, :, None, None] * dist)
    logits = jnp.where(valid[None, :, None, None], logits, -jnp.inf)
    sink_l = sink.astype(jnp.float32).reshape(GQA_KV_HEADS, GQA_GROUP)[None, None, :, :, None]
    m = jnp.maximum(logits.max(axis=-1), sink_l)
    e = jnp.exp(logits - m[..., None])
    denom = e.sum(axis=-1) + jnp.exp(sink_l - m)
    probs = (e / denom[..., None]).astype(v.dtype)
    o = jnp.einsum('bnkgqj,bnjkd->bnqkgd', probs, vw).reshape(bsz, s, GQA_HEADS * HEAD_DIM)
    return o @ w_o


def trunk(x, norm_g, ffn_w_gate, ffn_w_up, ffn_w_down, even_w_in, mla_g_q, mla_g_kv,
          mla_w_uq, mla_w_ukv, gm_g_v, gm_w_s, gm_b_s, even_w_o, odd_w_qkv, odd_sink, odd_w_o):
    for l in range(DEPTH):
        g = norm_g[l]
        x = x + 0.5 * rmsnorm(swiglu(rmsnorm(x, g[0]), ffn_w_gate[l, 0], ffn_w_up[l, 0], ffn_w_down[l, 0]), g[1])
        h = rmsnorm(x, g[2])
        i = l // 2
        if l % 2 == 0:
            m = even_mixer(h, even_w_in[i], mla_g_q[i], mla_g_kv[i], mla_w_uq[i], mla_w_ukv[i],
                           gm_g_v[i], gm_w_s[i], gm_b_s[i], even_w_o[i])
        else:
            m = window_gqa(h, odd_w_qkv[i], odd_sink[i], odd_w_o[i])
        x = x + rmsnorm(m, g[3])
        x = x + 0.5 * rmsnorm(swiglu(rmsnorm(x, g[4]), ffn_w_gate[l, 1], ffn_w_up[l, 1], ffn_w_down[l, 1]), g[5])
    return x


def setup_inputs(seed: int = 0) -> dict:
    key = jax.random.key(seed)
    ks = jax.random.split(key, 20)
    f32 = jnp.float32

    def nrm(k, shape, fan_in):
        return jax.random.normal(k, shape, f32) * (fan_in ** -0.5)

    def gain(k, shape):
        return 1.0 + 0.02 * jax.random.normal(k, shape, f32)

    return {
        "x_prompt": jax.random.normal(ks[0], (BATCH, SEQ, D_MODEL), f32),
        "x_sample": jax.random.normal(ks[1], (DEC_BATCH, DEC_SEQ, D_MODEL), f32),
        "norm_g": gain(ks[2], (DEPTH, N_NORMS, D_MODEL)),
        "ffn_w_gate": nrm(ks[3], (DEPTH, 2, D_MODEL, D_FF), D_MODEL),
        "ffn_w_up": nrm(ks[4], (DEPTH, 2, D_MODEL, D_FF), D_MODEL),
        "ffn_w_down": nrm(ks[5], (DEPTH, 2, D_FF, D_MODEL), D_FF),
        "even_w_in": nrm(ks[6], (N_EVEN, D_MODEL, IN_EVEN), D_MODEL),
        "mla_g_q": gain(ks[7], (N_EVEN, Q_LORA)),
        "mla_g_kv": gain(ks[8], (N_EVEN, KV_LORA)),
        "mla_w_uq": nrm(ks[9], (N_EVEN, Q_LORA, MLA_HEADS * (QK_NOPE + QK_ROPE)), Q_LORA),
        "mla_w_ukv": nrm(ks[10], (N_EVEN, KV_LORA, MLA_HEADS * (QK_NOPE + V_HEAD)), KV_LORA),
        "gm_g_v": gain(ks[11], (N_EVEN, GM_WIDTH)),
        "gm_w_s": nrm(ks[12], (N_EVEN, GM_GROUPS, GM_CHUNK, GM_CHUNK), GM_CHUNK),
        "gm_b_s": 1.0 + 0.01 * jax.random.normal(ks[13], (N_EVEN, GM_GROUPS, GM_CHUNK), f32),
        "even_w_o": nrm(ks[14], (N_EVEN, EVEN_MIX, D_MODEL), EVEN_MIX),
        "odd_w_qkv": nrm(ks[15], (N_ODD, D_MODEL, QKV_COLS), D_MODEL),
        "odd_sink": 0.5 * jax.random.normal(ks[16], (N_ODD, GQA_HEADS), f32),
        "odd_w_o": nrm(ks[17], (N_ODD, GQA_HEADS * HEAD_DIM, D_MODEL), GQA_HEADS * HEAD_DIM),
    }


def reference(x_prompt, x_sample, norm_g, ffn_w_gate, ffn_w_up, ffn_w_down, even_w_in, mla_g_q,
              mla_g_kv, mla_w_uq, mla_w_ukv, gm_g_v, gm_w_s, gm_b_s, even_w_o, odd_w_qkv,
              odd_sink, odd_w_o):
    y_prompt = trunk(x_prompt, norm_g, ffn_w_gate, ffn_w_up, ffn_w_down, even_w_in, mla_g_q, mla_g_kv,
                     mla_w_uq, mla_w_ukv, gm_g_v, gm_w_s, gm_b_s, even_w_o, odd_w_qkv, odd_sink, odd_w_o)
    y_sample = trunk(x_sample, norm_g, ffn_w_gate, ffn_w_up, ffn_w_down, even_w_in, mla_g_q, mla_g_kv,
                     mla_w_uq, mla_w_ukv, gm_g_v, gm_w_s, gm_b_s, even_w_o, odd_w_qkv, odd_sink, odd_w_o)
    return (y_prompt, y_sample)
```

```python
import functools

import jax
import jax.numpy as jnp
from jax import lax
from jax.experimental import pallas as pl
from jax.experimental.pallas import tpu as pltpu

F32 = jnp.float32
BF16 = jnp.bfloat16

EPS = 1e-6

MLA_HEADS = 8
Q_LORA = 768
KV_LORA = 512
QK_NOPE = 128
QK_ROPE = 64
V_HEAD = 128
ROPE_THETA = 10000.0
MLA_QK = QK_NOPE + 2 * QK_ROPE

GM_GROUPS = 8
GM_CHUNK = 128
GM_WIDTH = GM_GROUPS * 128

GQA_HEADS = 16
GQA_KV_HEADS = 4
GQA_GROUP = GQA_HEADS // GQA_KV_HEADS
HEAD_DIM = 128
WINDOW = 128

VMEM_LIMIT_BYTES = 56 * 1024 * 1024


def _params(*semantics):
    return pltpu.CompilerParams(dimension_semantics=semantics, vmem_limit_bytes=VMEM_LIMIT_BYTES)


def _resident(shape):
    zeros = (0,) * len(shape)
    return pl.BlockSpec(shape, lambda *_: zeros, pipeline_mode=pl.Buffered(1))


def _rms(x, g):
    return x * lax.rsqrt(jnp.mean(x * x, axis=-1, keepdims=True) + EPS) * g


def _ffn_body(x_ref, gin_ref, gout_ref, wg_ref, wu_ref, wd_ref, o_ref, xn_ref, acc_ref):
    j = pl.program_id(1)

    @pl.when(j == 0)
    def _():
        xn_ref[...] = _rms(x_ref[...], gin_ref[...]).astype(BF16)

    xn = xn_ref[...]
    gate = jnp.dot(xn, wg_ref[...], preferred_element_type=F32)
    up = jnp.dot(xn, wu_ref[...], preferred_element_type=F32)
    h = (gate * jax.nn.sigmoid(gate) * up).astype(BF16)
    part = jnp.dot(h, wd_ref[...], preferred_element_type=F32)

    @pl.when(j == 0)
    def _():
        acc_ref[...] = part

    @pl.when(j > 0)
    def _():
        acc_ref[...] += part

    @pl.when(j == pl.num_programs(1) - 1)
    def _():
        o_ref[...] = x_ref[...] + 0.5 * _rms(acc_ref[...], gout_ref[...])


def _ffn(x, g_in, g_out, w_gate, w_up, w_down, layer, idx, *, tm=512, tf=512):
    t, d = x.shape
    d_ff = w_gate.shape[-1]
    return pl.pallas_call(
        _ffn_body,
        grid=(t // tm, d_ff // tf),
        in_specs=[
            pl.BlockSpec((tm, d), lambda i, j: (i, 0)),
            pl.BlockSpec((1, d), lambda i, j: (0, 0)),
            pl.BlockSpec((1, d), lambda i, j: (0, 0)),
            pl.BlockSpec((None, None, d, tf), lambda i, j: (layer, idx, 0, j)),
            pl.BlockSpec((None, None, d, tf), lambda i, j: (layer, idx, 0, j)),
            pl.BlockSpec((None, None, tf, d), lambda i, j: (layer, idx, j, 0)),
        ],
        out_specs=pl.BlockSpec((tm, d), lambda i, j: (i, 0)),
        out_shape=jax.ShapeDtypeStruct((t, d), F32),
        scratch_shapes=[pltpu.VMEM((tm, d), BF16), pltpu.VMEM((tm, d), F32)],
        compiler_params=_params("parallel", "arbitrary"),
        name="ffn",
    )(x, g_in, g_out, w_gate, w_up, w_down)


def _rope_pair(blk, tab):
    a = blk * tab
    return a + pltpu.roll(a, QK_ROPE, axis=1)


def _even_in_body(x_ref, g_ref, win_ref, gq_ref, gkv_ref, wuq_ref, wukv_ref, gv_ref, tab_ref,
                  q_ref, k_ref, v_ref, u_ref, vl_ref):
    h = _rms(x_ref[...], g_ref[...]).astype(BF16)
    p = jnp.dot(h, win_ref[...], preferred_element_type=F32)
    o_kv = Q_LORA
    o_u = Q_LORA + KV_LORA
    o_v = o_u + GM_WIDTH
    o_r = o_v + GM_WIDTH
    tab = tab_ref[...]

    c_q = _rms(p[:, :o_kv], gq_ref[...]).astype(BF16)
    c_kv = _rms(p[:, o_kv:o_u], gkv_ref[...]).astype(BF16)
    q = jnp.dot(c_q, wuq_ref[...], preferred_element_type=F32)
    kv = jnp.dot(c_kv, wukv_ref[...], preferred_element_type=F32)

    scale = (QK_NOPE + QK_ROPE) ** -0.5
    k_rope = _rope_pair(p[:, o_r:o_r + 2 * QK_ROPE], tab)
    lane = lax.broadcasted_iota(jnp.int32, k_rope.shape, 1)
    k_rope = jnp.where(lane < QK_ROPE, k_rope, 0.0).astype(BF16)
    for hd in range(MLA_HEADS):
        qo = hd * MLA_QK
        q_ref[hd, :, :QK_NOPE] = (q[:, qo:qo + QK_NOPE] * scale).astype(BF16)
        q_rope = _rope_pair(q[:, qo + QK_NOPE:qo + MLA_QK], tab)
        q_ref[hd, :, QK_NOPE:] = (q_rope * scale).astype(BF16)
        ko = hd * (QK_NOPE + V_HEAD)
        k_ref[hd, :, :QK_NOPE] = kv[:, ko:ko + QK_NOPE].astype(BF16)
        k_ref[hd, :, QK_NOPE:] = k_rope
        v_ref[hd] = kv[:, ko + QK_NOPE:ko + QK_NOPE + V_HEAD].astype(BF16)

    u_ref[...] = jax.nn.gelu(p[:, o_u:o_v]).astype(BF16)
    gv = jax.nn.gelu(p[:, o_v:o_r])
    gc = gv - jnp.mean(gv, axis=-1, keepdims=True)
    ln = gc * lax.rsqrt(jnp.mean(gc * gc, axis=-1, keepdims=True) + EPS) * gv_ref[...]
    vl_ref[...] = ln.astype(BF16)


def _even_in(x, g, w_in, g_q, g_kv, w_uq, w_ukv, g_v, tab, seq, *, tm=256):
    t, d = x.shape
    tiles_per_seq = seq // tm
    tok = lambda i: (i, 0)
    head_tok = lambda i: (0, i, 0)
    return pl.pallas_call(
        _even_in_body,
        grid=(t // tm,),
        in_specs=[
            pl.BlockSpec((tm, d), tok),
            _resident(g.shape), _resident(w_in.shape), _resident(g_q.shape), _resident(g_kv.shape),
            _resident(w_uq.shape), _resident(w_ukv.shape), _resident(g_v.shape),
            pl.BlockSpec((tm, 2 * QK_ROPE), lambda i: (i % tiles_per_seq, 0)),
        ],
        out_specs=[
            pl.BlockSpec((MLA_HEADS, tm, MLA_QK), head_tok),
            pl.BlockSpec((MLA_HEADS, tm, MLA_QK), head_tok),
            pl.BlockSpec((MLA_HEADS, tm, V_HEAD), head_tok),
            pl.BlockSpec((tm, GM_WIDTH), tok),
            pl.BlockSpec((tm, GM_WIDTH), tok),
        ],
        out_shape=[
            jax.ShapeDtypeStruct((MLA_HEADS, t, MLA_QK), BF16),
            jax.ShapeDtypeStruct((MLA_HEADS, t, MLA_QK), BF16),
            jax.ShapeDtypeStruct((MLA_HEADS, t, V_HEAD), BF16),
            jax.ShapeDtypeStruct((t, GM_WIDTH), BF16),
            jax.ShapeDtypeStruct((t, GM_WIDTH), BF16),
        ],
        compiler_params=_params("parallel"),
        name="even_in",
    )(x, g, w_in, g_q, g_kv, w_uq, w_ukv, g_v, tab)


def _mla_body(q_ref, k_ref, v_ref, o_ref, m_ref, l_ref, acc_ref, *, tk):
    q = q_ref[...]
    m_ref[...] = jnp.full(m_ref.shape, -jnp.inf, F32)
    l_ref[...] = jnp.zeros(l_ref.shape, F32)
    acc_ref[...] = jnp.zeros(acc_ref.shape, F32)

    def step(c, carry):
        off = pl.multiple_of(c * tk, tk)
        k = k_ref[pl.ds(off, tk), :]
        v = v_ref[pl.ds(off, tk), :]
        s = lax.dot_general(q, k, (((1,), (1,)), ((), ())), preferred_element_type=F32)
        m_old = m_ref[...]
        m_new = jnp.maximum(m_old, jnp.max(s, axis=-1, keepdims=True))
        alpha = jnp.exp(m_old - m_new)
        p = jnp.exp(s - m_new)
        l_ref[...] = alpha * l_ref[...] + jnp.sum(p, axis=-1, keepdims=True)
        acc_ref[...] = alpha * acc_ref[...] + jnp.dot(p.astype(BF16), v, preferred_element_type=F32)
        m_ref[...] = m_new
        return carry

    lax.fori_loop(0, k_ref.shape[0] // tk, step, 0)
    o_ref[...] = (acc_ref[...] / l_ref[...]).astype(o_ref.dtype)


def _mla(q, k, v, n_seq, seq, *, tq=256, tk=512):
    t = n_seq * seq
    nq = seq // tq
    return pl.pallas_call(
        functools.partial(_mla_body, tk=tk),
        grid=(n_seq, MLA_HEADS, nq),
        in_specs=[
            pl.BlockSpec((None, tq, MLA_QK), lambda b, h, i: (h, b * nq + i, 0)),
            pl.BlockSpec((None, seq, MLA_QK), lambda b, h, i: (h, b, 0)),
            pl.BlockSpec((None, seq, V_HEAD), lambda b, h, i: (h, b, 0)),
        ],
        out_specs=pl.BlockSpec((tq, V_HEAD), lambda b, h, i: (b * nq + i, h)),
        out_shape=jax.ShapeDtypeStruct((t, MLA_HEADS * V_HEAD), BF16),
        scratch_shapes=[pltpu.VMEM((tq, 1), F32), pltpu.VMEM((tq, 1), F32), pltpu.VMEM((tq, V_HEAD), F32)],
        compiler_params=_params("parallel", "parallel", "arbitrary"),
        name="mla_attention",
    )(q, k, v)


def _even_out_body(x_ref, a_ref, u_ref, vl_ref, ws_ref, bs_ref, wo_ref, g_ref, o_ref, b_ref):
    tm = x_ref.shape[0]
    for c in range(tm // GM_CHUNK):
        rows = slice(c * GM_CHUNK, (c + 1) * GM_CHUNK)
        for grp in range(GM_GROUPS):
            cols = slice(grp * 128, (grp + 1) * 128)
            mixed = jnp.dot(ws_ref[grp], vl_ref[rows, cols], preferred_element_type=F32)
            mixed = mixed + bs_ref[:, grp:grp + 1]
            b_ref[rows, cols] = (u_ref[rows, cols].astype(F32) * mixed).astype(BF16)
    n_a = a_ref.shape[1]
    y = jnp.dot(a_ref[...], wo_ref[:n_a, :], preferred_element_type=F32)
    y = y + jnp.dot(b_ref[...], wo_ref[n_a:, :], preferred_element_type=F32)
    o_ref[...] = x_ref[...] + _rms(y, g_ref[...])


def _even_out(x, a, u, vl, w_s, b_s_t, w_o, g, *, tm=512):
    t, d = x.shape
    tok = lambda i: (i, 0)
    return pl.pallas_call(
        _even_out_body,
        grid=(t // tm,),
        in_specs=[
            pl.BlockSpec((tm, d), tok),
            pl.BlockSpec((tm, a.shape[1]), tok),
            pl.BlockSpec((tm, GM_WIDTH), tok),
            pl.BlockSpec((tm, GM_WIDTH), tok),
            _resident(w_s.shape), _resident(b_s_t.shape), _resident(w_o.shape), _resident(g.shape),
        ],
        out_specs=pl.BlockSpec((tm, d), tok),
        out_shape=jax.ShapeDtypeStruct((t, d), F32),
        scratch_shapes=[pltpu.VMEM((tm, GM_WIDTH), BF16)],
        compiler_params=_params("parallel"),
        name="even_out",
    )(x, a, u, vl, w_s, b_s_t, w_o, g)


def _odd_in_body(x_ref, g_ref, w_ref, q_ref, k_ref, v_ref):
    h = _rms(x_ref[...], g_ref[...]).astype(BF16)
    qkv = jnp.dot(h, w_ref[...], preferred_element_type=F32)
    nq = q_ref.shape[1]
    nk = k_ref.shape[1]
    q_ref[...] = (qkv[:, :nq] * HEAD_DIM ** -0.5).astype(BF16)
    k_ref[...] = qkv[:, nq:nq + nk].astype(BF16)
    v_ref[...] = qkv[:, nq + nk:].astype(BF16)


def _odd_in(x, g, w_qkv, *, tm=512):
    t, d = x.shape
    nq = GQA_HEADS * HEAD_DIM
    nk = GQA_KV_HEADS * HEAD_DIM
    tok = lambda i: (i, 0)
    return pl.pallas_call(
        _odd_in_body,
        grid=(t // tm,),
        in_specs=[pl.BlockSpec((tm, d), tok), _resident(g.shape), _resident(w_qkv.shape)],
        out_specs=[pl.BlockSpec((tm, nq), tok), pl.BlockSpec((tm, nk), tok), pl.BlockSpec((tm, nk), tok)],
        out_shape=[jax.ShapeDtypeStruct((t, nq), BF16), jax.ShapeDtypeStruct((t, nk), BF16),
                   jax.ShapeDtypeStruct((t, nk), BF16)],
        compiler_params=_params("parallel"),
        name="odd_in",
    )(x, g, w_qkv)


def _window_body(sink_ref, q_ref, kp_ref, km_ref, kn_ref, vp_ref, vm_ref, vn_ref, o_ref, k_scr, v_scr,
                 *, seq):
    blk = WINDOW
    tq = q_ref.shape[0]
    k_scr[:blk] = kp_ref[...]
    k_scr[blk:blk + tq] = km_ref[...]
    k_scr[blk + tq:] = kn_ref[...]
    v_scr[:blk] = vp_ref[...]
    v_scr[blk:blk + tq] = vm_ref[...]
    v_scr[blk + tq:] = vn_ref[...]

    a_idx = lax.broadcasted_iota(jnp.int32, (blk, 3 * blk), 0)
    j_idx = lax.broadcasted_iota(jnp.int32, (blk, 3 * blk), 1)
    dist = jnp.abs(j_idx - blk - a_idx)
    tile_start = pl.program_id(1) * tq

    for r in range(tq // blk):
        key_pos = tile_start + (r - 1) * blk + j_idx
        valid = (dist <= WINDOW) & (key_pos >= 0) & (key_pos < seq)
        neg_dist = jnp.where(valid, -dist.astype(F32), -jnp.inf)
        rows = slice(r * blk, (r + 1) * blk)
        win = slice(r * blk, (r + 3) * blk)
        for kvh in range(GQA_KV_HEADS):
            kcols = slice(kvh * HEAD_DIM, (kvh + 1) * HEAD_DIM)
            k_win = k_scr[win, kcols]
            v_win = v_scr[win, kcols]
            for grp in range(GQA_GROUP):
                head = kvh * GQA_GROUP + grp
                qcols = slice(head * HEAD_DIM, (head + 1) * HEAD_DIM)
                s = lax.dot_general(q_ref[rows, qcols], k_win, (((1,), (1,)), ((), ())),
                                    preferred_element_type=F32)
                slope = 2.0 ** (-8.0 * (head + 1) / GQA_HEADS)
                logits = s + slope * neg_dist
                sink = sink_ref[head]
                m = jnp.maximum(jnp.max(logits, axis=-1, keepdims=True), sink)
                e = jnp.exp(logits - m)
                denom = jnp.sum(e, axis=-1, keepdims=True) + jnp.exp(sink - m)
                o = jnp.dot(e.astype(BF16), v_win, preferred_element_type=F32)
                o_ref[rows, qcols] = (o / denom).astype(o_ref.dtype)


def _window(q, k, v, sink, n_seq, seq, *, tq=512):
    t = n_seq * seq
    nq = seq // tq
    r = tq // WINDOW
    nk = k.shape[1]
    blocks_per_seq = seq // WINDOW
    main = lambda b, i: (b * nq + i, 0)
    prev = lambda b, i: (b * blocks_per_seq + jnp.maximum(i * r - 1, 0), 0)
    nxt = lambda b, i: (b * blocks_per_seq + jnp.minimum((i + 1) * r, blocks_per_seq - 1), 0)
    kv_specs = [pl.BlockSpec((WINDOW, nk), prev), pl.BlockSpec((tq, nk), main), pl.BlockSpec((WINDOW, nk), nxt)]
    return pl.pallas_call(
        functools.partial(_window_body, seq=seq),
        grid=(n_seq, nq),
        in_specs=[pl.BlockSpec(memory_space=pltpu.SMEM), pl.BlockSpec((tq, q.shape[1]), main)]
        + kv_specs + kv_specs,
        out_specs=pl.BlockSpec((tq, q.shape[1]), main),
        out_shape=jax.ShapeDtypeStruct((t, q.shape[1]), BF16),
        scratch_shapes=[pltpu.VMEM((tq + 2 * WINDOW, nk), BF16), pltpu.VMEM((tq + 2 * WINDOW, nk), BF16)],
        compiler_params=_params("parallel", "parallel"),
        name="window_attention",
    )(sink, q, k, k, k, v, v, v)


def _odd_out_body(x_ref, a_ref, wo_ref, g_ref, o_ref):
    y = jnp.dot(a_ref[...], wo_ref[...], preferred_element_type=F32)
    o_ref[...] = x_ref[...] + _rms(y, g_ref[...])


def _odd_out(x, a, w_o, g, *, tm=512):
    t, d = x.shape
    tok = lambda i: (i, 0)
    return pl.pallas_call(
        _odd_out_body,
        grid=(t // tm,),
        in_specs=[pl.BlockSpec((tm, d), tok), pl.BlockSpec((tm, a.shape[1]), tok),
                  _resident(w_o.shape), _resident(g.shape)],
        out_specs=pl.BlockSpec((tm, d), tok),
        out_shape=jax.ShapeDtypeStruct((t, d), F32),
        compiler_params=_params("parallel"),
        name="odd_out",
    )(x, a, w_o, g)


def _swap_halves(w):
    half = w.shape[-1] // 2
    return jnp.concatenate([-w[..., half:], w[..., :half]], axis=-1)


def _prep_even(w_in, w_uq):
    o_u = Q_LORA + KV_LORA
    o_g = o_u + QK_ROPE
    k_r = w_in[:, o_u:o_g]
    w_in_x = jnp.concatenate([w_in[:, :o_u], w_in[:, o_g:], k_r, _swap_halves(k_r)], axis=-1)
    per_head = w_uq.reshape(w_uq.shape[0], MLA_HEADS, QK_NOPE + QK_ROPE)
    rope = per_head[..., QK_NOPE:]
    w_uq_x = jnp.concatenate([per_head, _swap_halves(rope)], axis=-1).reshape(w_uq.shape[0], MLA_HEADS * MLA_QK)
    return w_in_x.astype(BF16), w_uq_x.astype(BF16)


def _rope_table(seq):
    pos = jnp.arange(seq, dtype=F32)
    inv = ROPE_THETA ** (-jnp.arange(0, QK_ROPE, 2, dtype=F32) / QK_ROPE)
    ang = pos[:, None] * inv[None, :]
    cos, sin = jnp.cos(ang), jnp.sin(ang)
    return jnp.concatenate([cos, cos, sin, sin], axis=-1)


def _trunk(x3, w):
    n_seq, seq, d = x3.shape
    x = x3.reshape(n_seq * seq, d)
    depth = w["norm_g"].shape[0]
    for layer in range(depth):
        g = w["norm_g"][layer]
        row = lambda i: g[i][None, :]
        ffn = lambda x, n, idx: _ffn(x, row(n), row(n + 1), w["gate"], w["up"], w["down"], layer, idx)
        x = ffn(x, 0, 0)
        i = layer // 2
        if layer % 2 == 0:
            q, k, v, u, vl = _even_in(x, row(2), w["even_in"][i], w["g_q"][i][None, :], w["g_kv"][i][None, :],
                                      w["uq"][i], w["ukv"][i], w["g_v"][i][None, :], _rope_table(seq), seq)
            a = _mla(q, k, v, n_seq, seq)
            x = _even_out(x, a, u, vl, w["w_s"][i], w["b_s_t"][i], w["even_o"][i], row(3))
        else:
            q, k, v = _odd_in(x, row(2), w["qkv"][i])
            a = _window(q, k, v, w["sink"][i], n_seq, seq)
            x = _odd_out(x, a, w["odd_o"][i], row(3))
        x = ffn(x, 4, 1)
    return x.reshape(n_seq, seq, d)


def kernel(x_prompt, x_sample, norm_g, ffn_w_gate, ffn_w_up, ffn_w_down, even_w_in, mla_g_q, mla_g_kv, mla_w_uq, mla_w_ukv, gm_g_v, gm_w_s, gm_b_s, even_w_o, odd_w_qkv, odd_sink, odd_w_o):
    even = [_prep_even(even_w_in[i], mla_w_uq[i]) for i in range(even_w_in.shape[0])]
    w = {
        "norm_g": norm_g,
        "gate": ffn_w_gate.astype(BF16), "up": ffn_w_up.astype(BF16), "down": ffn_w_down.astype(BF16),
        "even_in": [e[0] for e in even], "uq": [e[1] for e in even],
        "g_q": mla_g_q, "g_kv": mla_g_kv, "ukv": mla_w_ukv.astype(BF16), "g_v": gm_g_v,
        "w_s": gm_w_s.astype(BF16), "b_s_t": jnp.swapaxes(gm_b_s, 1, 2), "even_o": even_w_o.astype(BF16),
        "qkv": odd_w_qkv.astype(BF16), "sink": odd_sink, "odd_o": odd_w_o.astype(BF16),
    }
    return (_trunk(x_prompt, w), _trunk(x_sample, w))
```

```python
import functools

import jax
import jax.numpy as jnp
from jax import lax
from jax.experimental import pallas as pl
from jax.experimental.pallas import tpu as pltpu

F32 = jnp.float32
BF16 = jnp.bfloat16

EPS = 1e-6

MLA_HEADS = 8
Q_LORA = 768
KV_LORA = 512
QK_NOPE = 128
QK_ROPE = 64
V_HEAD = 128
ROPE_THETA = 10000.0
MLA_QK = QK_NOPE + 2 * QK_ROPE
MLA_KV_CHUNK = 256

GM_GROUPS = 8
GM_CHUNK = 128
GM_WIDTH = GM_GROUPS * 128

GQA_HEADS = 16
GQA_KV_HEADS = 4
GQA_GROUP = GQA_HEADS // GQA_KV_HEADS
HEAD_DIM = 128
WINDOW = 128

FFN_ROW_SPLIT = 2

VMEM_LIMIT_BYTES = 56 * 1024 * 1024


def _params(*semantics):
    return pltpu.CompilerParams(dimension_semantics=semantics, vmem_limit_bytes=VMEM_LIMIT_BYTES)


def _resident(shape):
    zeros = (0,) * len(shape)
    return pl.BlockSpec(shape, lambda *_: zeros, pipeline_mode=pl.Buffered(1))


def _rms(x, g):
    return x * lax.rsqrt(jnp.mean(x * x, axis=-1, keepdims=True) + EPS) * g


def _ffn_body(x_ref, gin_ref, gout_ref, wg_ref, wu_ref, wd_ref, o_ref, xn_ref, acc_ref):
    j = pl.program_id(1)

    @pl.when(j == 0)
    def _():
        xn_ref[...] = _rms(x_ref[...], gin_ref[...]).astype(BF16)
        acc_ref[...] = jnp.zeros(acc_ref.shape, F32)

    tr = x_ref.shape[0] // FFN_ROW_SPLIT
    for r in range(FFN_ROW_SPLIT):
        rows = slice(r * tr, (r + 1) * tr)
        xn = xn_ref[rows, :]
        gate = jnp.dot(xn, wg_ref[...], preferred_element_type=F32)
        up = jnp.dot(xn, wu_ref[...], preferred_element_type=F32)
        h = (gate * jax.nn.sigmoid(gate) * up).astype(BF16)
        acc_ref[rows, :] += jnp.dot(h, wd_ref[...], preferred_element_type=F32)

    @pl.when(j == pl.num_programs(1) - 1)
    def _():
        o_ref[...] = x_ref[...] + 0.5 * _rms(acc_ref[...], gout_ref[...])


def _ffn(x, g_in, g_out, w_gate, w_up, w_down, layer, idx, *, tm=512, tf=512):
    t, d = x.shape
    d_ff = w_gate.shape[-1]
    return pl.pallas_call(
        _ffn_body,
        grid=(t // tm, d_ff // tf),
        in_specs=[
            pl.BlockSpec((tm, d), lambda i, j: (i, 0)),
            pl.BlockSpec((1, d), lambda i, j: (0, 0)),
            pl.BlockSpec((1, d), lambda i, j: (0, 0)),
            pl.BlockSpec((None, None, d, tf), lambda i, j: (layer, idx, 0, j)),
            pl.BlockSpec((None, None, d, tf), lambda i, j: (layer, idx, 0, j)),
            pl.BlockSpec((None, None, tf, d), lambda i, j: (layer, idx, j, 0)),
        ],
        out_specs=pl.BlockSpec((tm, d), lambda i, j: (i, 0)),
        out_shape=jax.ShapeDtypeStruct((t, d), F32),
        scratch_shapes=[pltpu.VMEM((tm, d), BF16), pltpu.VMEM((tm, d), F32)],
        compiler_params=_params("parallel", "arbitrary"),
        name="ffn",
    )(x, g_in, g_out, w_gate, w_up, w_down)


_NT = (((1,), (1,)), ((), ()))


def _even_in_body(x_ref, g_ref, win_ref, gq_ref, gkv_ref, wuqt_ref, wuk_ref, wuvt_ref, gv_ref, tab_ref,
                  tabt_ref, qt_ref, k_ref, vt_ref, u_ref, vl_ref):
    h = _rms(x_ref[...], g_ref[...]).astype(BF16)
    p = jnp.dot(h, win_ref[...], preferred_element_type=F32)
    o_kv = Q_LORA
    o_u = Q_LORA + KV_LORA
    o_v = o_u + GM_WIDTH
    o_r = o_v + GM_WIDTH

    c_q = _rms(p[:, :o_kv], gq_ref[...]).astype(BF16)
    c_kv = _rms(p[:, o_kv:o_u], gkv_ref[...]).astype(BF16)
    qt = lax.dot_general(wuqt_ref[...], c_q, _NT, preferred_element_type=F32)
    vt = lax.dot_general(wuvt_ref[...], c_kv, _NT, preferred_element_type=F32)
    k_nope = jnp.dot(c_kv, wuk_ref[...], preferred_element_type=F32)

    a = p[:, o_r:o_r + 2 * QK_ROPE] * tab_ref[...]
    k_rope = (a + pltpu.roll(a, QK_ROPE, axis=1)).astype(BF16)
    tabt = tabt_ref[...]
    q_scale = (QK_NOPE + QK_ROPE) ** -0.5 * 1.4426950408889634
    tm = x_ref.shape[0]
    for hd in range(MLA_HEADS):
        qo = hd * MLA_QK
        qt_ref[hd, :QK_NOPE, :] = (qt[qo:qo + QK_NOPE] * q_scale).astype(BF16)
        ar = qt[qo + QK_NOPE:qo + MLA_QK] * tabt
        qt_ref[hd, QK_NOPE:QK_NOPE + QK_ROPE, :] = ((ar[:QK_ROPE] + ar[QK_ROPE:]) * q_scale).astype(BF16)
        qt_ref[hd, QK_NOPE + QK_ROPE:, :] = jnp.zeros((QK_ROPE, tm), BF16)
        k_ref[hd, :, :QK_NOPE] = k_nope[:, hd * QK_NOPE:(hd + 1) * QK_NOPE].astype(BF16)
        k_ref[hd, :, QK_NOPE:] = k_rope
        for c in range(tm // MLA_KV_CHUNK):
            vt_ref[hd, c] = vt[hd * V_HEAD:(hd + 1) * V_HEAD,
                               c * MLA_KV_CHUNK:(c + 1) * MLA_KV_CHUNK].astype(BF16)

    u_ref[...] = jax.nn.gelu(p[:, o_u:o_v]).astype(BF16)
    gv = jax.nn.gelu(p[:, o_v:o_r])
    gc = gv - jnp.mean(gv, axis=-1, keepdims=True)
    ln = gc * lax.rsqrt(jnp.mean(gc * gc, axis=-1, keepdims=True) + EPS) * gv_ref[...]
    vl_ref[...] = ln.astype(BF16)


def _even_in(x, g, w_in, g_q, g_kv, w_uq_t, w_uk, w_uv_t, g_v, tab, seq, *, tm=256):
    t, d = x.shape
    tiles_per_seq = seq // tm
    tok = lambda i: (i, 0)
    consts = [g, w_in, g_q, g_kv, w_uq_t, w_uk, w_uv_t, g_v]
    return pl.pallas_call(
        _even_in_body,
        grid=(t // tm,),
        in_specs=[pl.BlockSpec((tm, d), tok)] + [_resident(c.shape) for c in consts] + [
            pl.BlockSpec((tm, 2 * QK_ROPE), lambda i: (i % tiles_per_seq, 0)),
            pl.BlockSpec((2 * QK_ROPE, tm), lambda i: (0, i % tiles_per_seq)),
        ],
        out_specs=[
            pl.BlockSpec((MLA_HEADS, MLA_QK, tm), lambda i: (0, 0, i)),
            pl.BlockSpec((MLA_HEADS, tm, MLA_QK), lambda i: (0, i, 0)),
            pl.BlockSpec((MLA_HEADS, tm // MLA_KV_CHUNK, V_HEAD, MLA_KV_CHUNK), lambda i: (0, i, 0, 0)),
            pl.BlockSpec((tm, GM_WIDTH), tok),
            pl.BlockSpec((tm, GM_WIDTH), tok),
        ],
        out_shape=[
            jax.ShapeDtypeStruct((MLA_HEADS, MLA_QK, t), BF16),
            jax.ShapeDtypeStruct((MLA_HEADS, t, MLA_QK), BF16),
            jax.ShapeDtypeStruct((MLA_HEADS, t // MLA_KV_CHUNK, V_HEAD, MLA_KV_CHUNK), BF16),
            jax.ShapeDtypeStruct((t, GM_WIDTH), BF16),
            jax.ShapeDtypeStruct((t, GM_WIDTH), BF16),
        ],
        compiler_params=_params("parallel"),
        name="even_in",
    )(x, *consts, tab, tab.T)


def _mla_body(qt_ref, k_ref, vt_ref, o_ref, acc_ref, s0_ref, s1_ref, *, tk):
    qt = qt_ref[...]
    tq = qt.shape[1]
    sub = tk // MLA_KV_CHUNK
    n_chunks = k_ref.shape[0] // tk
    acc_ref[...] = jnp.zeros(acc_ref.shape, F32)

    def scores(c, s_ref):
        off = pl.multiple_of(c * tk, tk)
        s_ref[...] = jnp.dot(k_ref[pl.ds(off, tk), :], qt, preferred_element_type=F32)

    def absorb(c, s_ref, m_old, l_old):
        st = s_ref[...]
        m_new = jnp.maximum(m_old, jnp.max(st, axis=0, keepdims=True))
        alpha = jnp.exp2(m_old - m_new)
        pt = jnp.exp2(st - m_new)
        l_new = alpha * l_old + jnp.sum(pt, axis=0, keepdims=True)
        pb = pt.astype(BF16)
        pv = jnp.dot(vt_ref[c * sub], pb[:MLA_KV_CHUNK], preferred_element_type=F32)
        for j in range(1, sub):
            pv = pv + jnp.dot(vt_ref[c * sub + j], pb[j * MLA_KV_CHUNK:(j + 1) * MLA_KV_CHUNK],
                              preferred_element_type=F32)
        acc_ref[...] = alpha * acc_ref[...] + pv
        return m_new, l_new

    def step(c2, carry):
        m, l = carry
        c = 2 * c2
        scores(c + 1, s1_ref)
        m, l = absorb(c, s0_ref, m, l)
        scores(jnp.minimum(c + 2, n_chunks - 1), s0_ref)
        return absorb(c + 1, s1_ref, m, l)

    scores(0, s0_ref)
    init = (jnp.full((1, tq), -jnp.inf, F32), jnp.zeros((1, tq), F32))
    _, l = lax.fori_loop(0, n_chunks // 2, step, init)
    o_ref[...] = (acc_ref[...] / l).T.astype(o_ref.dtype)


def _mla(qt, k, vt, n_seq, seq, *, tq=512, tk=512):
    t = n_seq * seq
    nq = seq // tq
    assert seq % tq == 0 and seq % (2 * tk) == 0, "key chunks are consumed in pairs"
    return pl.pallas_call(
        functools.partial(_mla_body, tk=tk),
        grid=(n_seq, MLA_HEADS, nq),
        in_specs=[
            pl.BlockSpec((None, MLA_QK, tq), lambda b, h, i: (h, 0, b * nq + i)),
            pl.BlockSpec((None, seq, MLA_QK), lambda b, h, i: (h, b, 0)),
            pl.BlockSpec((None, seq // MLA_KV_CHUNK, V_HEAD, MLA_KV_CHUNK), lambda b, h, i: (h, b, 0, 0)),
        ],
        out_specs=pl.BlockSpec((tq, V_HEAD), lambda b, h, i: (b * nq + i, h)),
        out_shape=jax.ShapeDtypeStruct((t, MLA_HEADS * V_HEAD), BF16),
        scratch_shapes=[pltpu.VMEM((V_HEAD, tq), F32), pltpu.VMEM((tk, tq), F32), pltpu.VMEM((tk, tq), F32)],
        compiler_params=_params("parallel", "parallel", "arbitrary"),
        name="mla_attention",
    )(qt, k, vt)


def _even_out_body(x_ref, a_ref, u_ref, vl_ref, ws_ref, bs_ref, wo_ref, g_ref, o_ref, b_ref):
    tm = x_ref.shape[0]
    for c in range(tm // GM_CHUNK):
        rows = slice(c * GM_CHUNK, (c + 1) * GM_CHUNK)
        for grp in range(GM_GROUPS):
            cols = slice(grp * 128, (grp + 1) * 128)
            mixed = jnp.dot(ws_ref[grp], vl_ref[rows, cols], preferred_element_type=F32)
            mixed = mixed + bs_ref[:, grp:grp + 1]
            b_ref[rows, cols] = (u_ref[rows, cols].astype(F32) * mixed).astype(BF16)
    n_a = a_ref.shape[1]
    y = jnp.dot(a_ref[...], wo_ref[:n_a, :], preferred_element_type=F32)
    y = y + jnp.dot(b_ref[...], wo_ref[n_a:, :], preferred_element_type=F32)
    o_ref[...] = x_ref[...] + _rms(y, g_ref[...])


def _even_out(x, a, u, vl, w_s, b_s_t, w_o, g, *, tm=512):
    t, d = x.shape
    tok = lambda i: (i, 0)
    return pl.pallas_call(
        _even_out_body,
        grid=(t // tm,),
        in_specs=[
            pl.BlockSpec((tm, d), tok),
            pl.BlockSpec((tm, a.shape[1]), tok),
            pl.BlockSpec((tm, GM_WIDTH), tok),
            pl.BlockSpec((tm, GM_WIDTH), tok),
            _resident(w_s.shape), _resident(b_s_t.shape), _resident(w_o.shape), _resident(g.shape),
        ],
        out_specs=pl.BlockSpec((tm, d), tok),
        out_shape=jax.ShapeDtypeStruct((t, d), F32),
        scratch_shapes=[pltpu.VMEM((tm, GM_WIDTH), BF16)],
        compiler_params=_params("parallel"),
        name="even_out",
    )(x, a, u, vl, w_s, b_s_t, w_o, g)


def _odd_in_body(x_ref, g_ref, w_ref, q_ref, k_ref, v_ref):
    h = _rms(x_ref[...], g_ref[...]).astype(BF16)
    qkv = jnp.dot(h, w_ref[...], preferred_element_type=F32)
    nq = q_ref.shape[1]
    nk = k_ref.shape[1]
    q_ref[...] = (qkv[:, :nq] * HEAD_DIM ** -0.5).astype(BF16)
    k_ref[...] = qkv[:, nq:nq + nk].astype(BF16)
    v_ref[...] = qkv[:, nq + nk:].astype(BF16)


def _odd_in(x, g, w_qkv, *, tm=512):
    t, d = x.shape
    nq = GQA_HEADS * HEAD_DIM
    nk = GQA_KV_HEADS * HEAD_DIM
    tok = lambda i: (i, 0)
    return pl.pallas_call(
        _odd_in_body,
        grid=(t // tm,),
        in_specs=[pl.BlockSpec((tm, d), tok), _resident(g.shape), _resident(w_qkv.shape)],
        out_specs=[pl.BlockSpec((tm, nq), tok), pl.BlockSpec((tm, nk), tok), pl.BlockSpec((tm, nk), tok)],
        out_shape=[jax.ShapeDtypeStruct((t, nq), BF16), jax.ShapeDtypeStruct((t, nk), BF16),
                   jax.ShapeDtypeStruct((t, nk), BF16)],
        compiler_params=_params("parallel"),
        name="odd_in",
    )(x, g, w_qkv)


def _window_body(sink_ref, q_ref, kp_ref, km_ref, kn_ref, vp_ref, vm_ref, vn_ref, o_ref, k_scr, v_scr,
                 *, seq):
    blk = WINDOW
    tq = q_ref.shape[0]
    k_scr[:blk] = kp_ref[...]
    k_scr[blk:blk + tq] = km_ref[...]
    k_scr[blk + tq:] = kn_ref[...]
    v_scr[:blk] = vp_ref[...]
    v_scr[blk:blk + tq] = vm_ref[...]
    v_scr[blk + tq:] = vn_ref[...]

    a_idx = lax.broadcasted_iota(jnp.int32, (blk, 3 * blk), 0)
    j_idx = lax.broadcasted_iota(jnp.int32, (blk, 3 * blk), 1)
    dist = jnp.abs(j_idx - blk - a_idx)
    tile_start = pl.program_id(1) * tq

    for r in range(tq // blk):
        key_pos = tile_start + (r - 1) * blk + j_idx
        valid = (dist <= WINDOW) & (key_pos >= 0) & (key_pos < seq)
        neg_dist = jnp.where(valid, -dist.astype(F32), -jnp.inf)
        rows = slice(r * blk, (r + 1) * blk)
        win = slice(r * blk, (r + 3) * blk)
        for kvh in range(GQA_KV_HEADS):
            kcols = slice(kvh * HEAD_DIM, (kvh + 1) * HEAD_DIM)
            k_win = k_scr[win, kcols]
            v_win = v_scr[win, kcols]
            for grp in range(GQA_GROUP):
                head = kvh * GQA_GROUP + grp
                qcols = slice(head * HEAD_DIM, (head + 1) * HEAD_DIM)
                s = lax.dot_general(q_ref[rows, qcols], k_win, (((1,), (1,)), ((), ())),
                                    preferred_element_type=F32)
                slope = 2.0 ** (-8.0 * (head + 1) / GQA_HEADS)
                logits = s + slope * neg_dist
                sink = sink_ref[head]
                m = jnp.maximum(jnp.max(logits, axis=-1, keepdims=True), sink)
                e = jnp.exp(logits - m)
                denom = jnp.sum(e, axis=-1, keepdims=True) + jnp.exp(sink - m)
                o = jnp.dot(e.astype(BF16), v_win, preferred_element_type=F32)
                o_ref[rows, qcols] = (o / denom).astype(o_ref.dtype)


def _window(q, k, v, sink, n_seq, seq, *, tq=512):
    t = n_seq * seq
    nq = seq // tq
    r = tq // WINDOW
    nk = k.shape[1]
    blocks_per_seq = seq // WINDOW
    main = lambda b, i: (b * nq + i, 0)
    prev = lambda b, i: (b * blocks_per_seq + jnp.maximum(i * r - 1, 0), 0)
    nxt = lambda b, i: (b * blocks_per_seq + jnp.minimum((i + 1) * r, blocks_per_seq - 1), 0)
    kv_specs = [pl.BlockSpec((WINDOW, nk), prev), pl.BlockSpec((tq, nk), main), pl.BlockSpec((WINDOW, nk), nxt)]
    return pl.pallas_call(
        functools.partial(_window_body, seq=seq),
        grid=(n_seq, nq),
        in_specs=[pl.BlockSpec(memory_space=pltpu.SMEM), pl.BlockSpec((tq, q.shape[1]), main)]
        + kv_specs + kv_specs,
        out_specs=pl.BlockSpec((tq, q.shape[1]), main),
        out_shape=jax.ShapeDtypeStruct((t, q.shape[1]), BF16),
        scratch_shapes=[pltpu.VMEM((tq + 2 * WINDOW, nk), BF16), pltpu.VMEM((tq + 2 * WINDOW, nk), BF16)],
        compiler_params=_params("parallel", "parallel"),
        name="window_attention",
    )(sink, q, k, k, k, v, v, v)


def _odd_out_body(x_ref, a_ref, wo_ref, g_ref, o_ref):
    y = jnp.dot(a_ref[...], wo_ref[...], preferred_element_type=F32)
    o_ref[...] = x_ref[...] + _rms(y, g_ref[...])


def _odd_out(x, a, w_o, g, *, tm=512):
    t, d = x.shape
    tok = lambda i: (i, 0)
    return pl.pallas_call(
        _odd_out_body,
        grid=(t // tm,),
        in_specs=[pl.BlockSpec((tm, d), tok), pl.BlockSpec((tm, a.shape[1]), tok),
                  _resident(w_o.shape), _resident(g.shape)],
        out_specs=pl.BlockSpec((tm, d), tok),
        out_shape=jax.ShapeDtypeStruct((t, d), F32),
        compiler_params=_params("parallel"),
        name="odd_out",
    )(x, a, w_o, g)


def _swap_halves(w):
    half = w.shape[-1] // 2
    return jnp.concatenate([-w[..., half:], w[..., :half]], axis=-1)


def _prep_even(w_in, w_uq, w_ukv):
    o_u = Q_LORA + KV_LORA
    o_g = o_u + QK_ROPE
    k_r = w_in[:, o_u:o_g]
    w_in_x = jnp.concatenate([w_in[:, :o_u], w_in[:, o_g:], k_r, _swap_halves(k_r)], axis=-1)
    per_head = w_uq.reshape(w_uq.shape[0], MLA_HEADS, QK_NOPE + QK_ROPE)
    rope = per_head[..., QK_NOPE:]
    w_uq_x = jnp.concatenate([per_head, _swap_halves(rope)], axis=-1).reshape(w_uq.shape[0], MLA_HEADS * MLA_QK)
    kv = w_ukv.reshape(w_ukv.shape[0], MLA_HEADS, QK_NOPE + V_HEAD)
    w_uk = kv[..., :QK_NOPE].reshape(w_ukv.shape[0], MLA_HEADS * QK_NOPE)
    w_uv = kv[..., QK_NOPE:].reshape(w_ukv.shape[0], MLA_HEADS * V_HEAD)
    return w_in_x.astype(BF16), w_uq_x.T.astype(BF16), w_uk.astype(BF16), w_uv.T.astype(BF16)


def _rope_table(seq):
    pos = jnp.arange(seq, dtype=F32)
    inv = ROPE_THETA ** (-jnp.arange(0, QK_ROPE, 2, dtype=F32) / QK_ROPE)
    ang = pos[:, None] * inv[None, :]
    cos, sin = jnp.cos(ang), jnp.sin(ang)
    return jnp.concatenate([cos, cos, sin, sin], axis=-1)


def _trunk(x3, w):
    n_seq, seq, d = x3.shape
    x = x3.reshape(n_seq * seq, d)
    depth = w["norm_g"].shape[0]
    for layer in range(depth):
        g = w["norm_g"][layer]
        row = lambda i: g[i][None, :]
        ffn = lambda x, n, idx: _ffn(x, row(n), row(n + 1), w["gate"], w["up"], w["down"], layer, idx)
        x = ffn(x, 0, 0)
        i = layer // 2
        if layer % 2 == 0:
            w_in, w_uq_t, w_uk, w_uv_t = w["even"][i]
            qt, k, vt, u, vl = _even_in(x, row(2), w_in, w["g_q"][i][None, :], w["g_kv"][i][None, :],
                                        w_uq_t, w_uk, w_uv_t, w["g_v"][i][None, :], _rope_table(seq), seq)
            a = _mla(qt, k, vt, n_seq, seq)
            x = _even_out(x, a, u, vl, w["w_s"][i], w["b_s_t"][i], w["even_o"][i], row(3))
        else:
            q, k, v = _odd_in(x, row(2), w["qkv"][i])
            a = _window(q, k, v, w["sink"][i], n_seq, seq)
            x = _odd_out(x, a, w["odd_o"][i], row(3))
        x = ffn(x, 4, 1)
    return x.reshape(n_seq, seq, d)


def kernel(x_prompt, x_sample, norm_g, ffn_w_gate, ffn_w_up, ffn_w_down, even_w_in, mla_g_q, mla_g_kv, mla_w_uq, mla_w_ukv, gm_g_v, gm_w_s, gm_b_s, even_w_o, odd_w_qkv, odd_sink, odd_w_o):
    even = [_prep_even(even_w_in[i], mla_w_uq[i], mla_w_ukv[i]) for i in range(even_w_in.shape[0])]
    w = {
        "norm_g": norm_g,
        "gate": ffn_w_gate.astype(BF16), "up": ffn_w_up.astype(BF16), "down": ffn_w_down.astype(BF16),
        "even": even, "g_q": mla_g_q, "g_kv": mla_g_kv, "g_v": gm_g_v,
        "w_s": gm_w_s.astype(BF16), "b_s_t": jnp.swapaxes(gm_b_s, 1, 2), "even_o": even_w_o.astype(BF16),
        "qkv": odd_w_qkv.astype(BF16), "sink": odd_sink, "odd_o": odd_w_o.astype(BF16),
    }
    return (_trunk(x_prompt, w), _trunk(x_sample, w))
```

```python
import functools

import jax
import jax.numpy as jnp
from jax import lax
from jax.experimental import pallas as pl
from jax.experimental.pallas import tpu as pltpu

F32 = jnp.float32
BF16 = jnp.bfloat16

EPS = 1e-6
LOG2E = 1.4426950408889634

MLA_HEADS = 8
Q_LORA = 768
KV_LORA = 512
QK_NOPE = 128
QK_ROPE = 64
V_HEAD = 128
ROPE_THETA = 10000.0
MLA_QK = QK_NOPE + 2 * QK_ROPE
MLA_KV_CHUNK = 256
MLA_UNROLL = 4

GM_GROUPS = 8
GM_CHUNK = 128
GM_WIDTH = GM_GROUPS * 128

GQA_HEADS = 16
GQA_KV_HEADS = 4
GQA_GROUP = GQA_HEADS // GQA_KV_HEADS
HEAD_DIM = 128
WINDOW = 128

FFN_ROW_SPLIT = 2
FFN_CHUNK = 512

VMEM_LIMIT_BYTES = 56 * 1024 * 1024


def _params(*semantics):
    return pltpu.CompilerParams(dimension_semantics=semantics, vmem_limit_bytes=VMEM_LIMIT_BYTES)


def _resident(shape):
    zeros = (0,) * len(shape)
    return pl.BlockSpec(shape, lambda *_: zeros, pipeline_mode=pl.Buffered(1))


def _rms(x, g):
    return x * lax.rsqrt(jnp.mean(x * x, axis=-1, keepdims=True) + EPS) * g


def _ffn_body(x_ref, gin_ref, gout_ref, wgu_ref, wd_ref, o_ref, xn_ref, acc_ref):
    j = pl.program_id(1)
    tf = wd_ref.shape[0]

    @pl.when(j == 0)
    def _():
        xn_ref[...] = _rms(x_ref[...], gin_ref[...]).astype(BF16)
        acc_ref[...] = jnp.zeros(acc_ref.shape, F32)

    tr = x_ref.shape[0] // FFN_ROW_SPLIT
    for r in range(FFN_ROW_SPLIT):
        rows = slice(r * tr, (r + 1) * tr)
        xn = xn_ref[rows, :]
        gu = jnp.dot(xn, wgu_ref[...], preferred_element_type=F32)
        gate, up = gu[:, :tf], gu[:, tf:]
        h = (gate * jax.nn.sigmoid(gate) * up).astype(BF16)
        acc_ref[rows, :] += jnp.dot(h, wd_ref[...], preferred_element_type=F32)

    @pl.when(j == pl.num_programs(1) - 1)
    def _():
        o_ref[...] = x_ref[...] + 0.5 * _rms(acc_ref[...], gout_ref[...])


def _ffn(x, g_in, g_out, w_gu, w_down, layer, idx, *, tm=512):
    t, d = x.shape
    n_chunks, tf = w_gu.shape[2], w_gu.shape[4] // 2
    return pl.pallas_call(
        _ffn_body,
        grid=(t // tm, n_chunks),
        in_specs=[
            pl.BlockSpec((tm, d), lambda i, j: (i, 0)),
            pl.BlockSpec((1, d), lambda i, j: (0, 0)),
            pl.BlockSpec((1, d), lambda i, j: (0, 0)),
            pl.BlockSpec((None, None, None, d, 2 * tf), lambda i, j: (layer, idx, j, 0, 0)),
            pl.BlockSpec((None, None, tf, d), lambda i, j: (layer, idx, j, 0)),
        ],
        out_specs=pl.BlockSpec((tm, d), lambda i, j: (i, 0)),
        out_shape=jax.ShapeDtypeStruct((t, d), F32),
        scratch_shapes=[pltpu.VMEM((tm, d), BF16), pltpu.VMEM((tm, d), F32)],
        compiler_params=_params("parallel", "arbitrary"),
        name="ffn",
    )(x, g_in, g_out, w_gu, w_down)


def _prep_ffn(w_gate, w_up, tf):
    depth, two, d, d_ff = w_gate.shape
    split = lambda w: w.reshape(depth, two, d, d_ff // tf, tf)
    gu = jnp.concatenate([split(w_gate), split(w_up)], axis=-1)
    return jnp.transpose(gu, (0, 1, 3, 2, 4)).astype(BF16)


_NT = (((1,), (1,)), ((), ()))


def _even_in_body(x_ref, g_ref, win_ref, gq_ref, gkv_ref, wuqt_ref, wuk_ref, wuvt_ref, gv_ref, tab_ref,
                  tabt_ref, qt_ref, k_ref, vt_ref, u_ref, vl_ref):
    h = _rms(x_ref[...], g_ref[...]).astype(BF16)
    p = jnp.dot(h, win_ref[...], preferred_element_type=F32)
    o_kv = Q_LORA
    o_u = Q_LORA + KV_LORA
    o_v = o_u + GM_WIDTH
    o_r = o_v + GM_WIDTH

    c_q = _rms(p[:, :o_kv], gq_ref[...]).astype(BF16)
    c_kv = _rms(p[:, o_kv:o_u], gkv_ref[...]).astype(BF16)
    qt = lax.dot_general(wuqt_ref[...], c_q, _NT, preferred_element_type=F32)
    vt = lax.dot_general(wuvt_ref[...], c_kv, _NT, preferred_element_type=F32)
    k_nope = jnp.dot(c_kv, wuk_ref[...], preferred_element_type=F32)

    a = p[:, o_r:o_r + 2 * QK_ROPE] * tab_ref[...]
    k_rope = (a + pltpu.roll(a, QK_ROPE, axis=1)).astype(BF16)
    tabt = tabt_ref[...]
    q_scale = (QK_NOPE + QK_ROPE) ** -0.5 * LOG2E
    tm = x_ref.shape[0]
    for hd in range(MLA_HEADS):
        qo = hd * MLA_QK
        qt_ref[hd, :QK_NOPE, :] = (qt[qo:qo + QK_NOPE] * q_scale).astype(BF16)
        ar = qt[qo + QK_NOPE:qo + MLA_QK] * tabt
        qt_ref[hd, QK_NOPE:QK_NOPE + QK_ROPE, :] = ((ar[:QK_ROPE] + ar[QK_ROPE:]) * q_scale).astype(BF16)
        qt_ref[hd, QK_NOPE + QK_ROPE:, :] = jnp.zeros((QK_ROPE, tm), BF16)
        k_ref[hd, :, :QK_NOPE] = k_nope[:, hd * QK_NOPE:(hd + 1) * QK_NOPE].astype(BF16)
        k_ref[hd, :, QK_NOPE:] = k_rope
        for c in range(tm // MLA_KV_CHUNK):
            vt_ref[hd, c] = vt[hd * V_HEAD:(hd + 1) * V_HEAD,
                               c * MLA_KV_CHUNK:(c + 1) * MLA_KV_CHUNK].astype(BF16)

    u_ref[...] = jax.nn.gelu(p[:, o_u:o_v]).astype(BF16)
    gv = jax.nn.gelu(p[:, o_v:o_r])
    gc = gv - jnp.mean(gv, axis=-1, keepdims=True)
    ln = gc * lax.rsqrt(jnp.mean(gc * gc, axis=-1, keepdims=True) + EPS) * gv_ref[...]
    vl_ref[...] = ln.astype(BF16)


def _even_in(x, g, w_in, g_q, g_kv, w_uq_t, w_uk, w_uv_t, g_v, tab, seq, *, tm=256):
    t, d = x.shape
    tiles_per_seq = seq // tm
    tok = lambda i: (i, 0)
    consts = [g, w_in, g_q, g_kv, w_uq_t, w_uk, w_uv_t, g_v]
    return pl.pallas_call(
        _even_in_body,
        grid=(t // tm,),
        in_specs=[pl.BlockSpec((tm, d), tok)] + [_resident(c.shape) for c in consts] + [
            pl.BlockSpec((tm, 2 * QK_ROPE), lambda i: (i % tiles_per_seq, 0)),
            pl.BlockSpec((2 * QK_ROPE, tm), lambda i: (0, i % tiles_per_seq)),
        ],
        out_specs=[
            pl.BlockSpec((MLA_HEADS, MLA_QK, tm), lambda i: (0, 0, i)),
            pl.BlockSpec((MLA_HEADS, tm, MLA_QK), lambda i: (0, i, 0)),
            pl.BlockSpec((MLA_HEADS, tm // MLA_KV_CHUNK, V_HEAD, MLA_KV_CHUNK), lambda i: (0, i, 0, 0)),
            pl.BlockSpec((tm, GM_WIDTH), tok),
            pl.BlockSpec((tm, GM_WIDTH), tok),
        ],
        out_shape=[
            jax.ShapeDtypeStruct((MLA_HEADS, MLA_QK, t), BF16),
            jax.ShapeDtypeStruct((MLA_HEADS, t, MLA_QK), BF16),
            jax.ShapeDtypeStruct((MLA_HEADS, t // MLA_KV_CHUNK, V_HEAD, MLA_KV_CHUNK), BF16),
            jax.ShapeDtypeStruct((t, GM_WIDTH), BF16),
            jax.ShapeDtypeStruct((t, GM_WIDTH), BF16),
        ],
        compiler_params=_params("parallel"),
        name="even_in",
    )(x, *consts, tab, tab.T)


def _mla_body(qt_ref, k_ref, vt_ref, o_ref, acc_ref, s_ref, *, tk):
    qt = qt_ref[...]
    tq = qt.shape[1]
    sub = tk // MLA_KV_CHUNK
    n_chunks = k_ref.shape[0] // tk
    acc_ref[...] = jnp.zeros(acc_ref.shape, F32)

    def scores(c, slot):
        off = pl.multiple_of(c * tk, tk)
        s_ref[slot] = jnp.dot(k_ref[pl.ds(off, tk), :], qt, preferred_element_type=F32)

    def absorb(c, slot, m_old, l_old):
        st = s_ref[slot]
        m_new = jnp.maximum(m_old, jnp.max(st, axis=0, keepdims=True))
        alpha = jnp.exp2(m_old - m_new)
        pt = jnp.exp2(st - m_new)
        l_new = alpha * l_old + jnp.sum(pt, axis=0, keepdims=True)
        pb = pt.astype(BF16)
        pv = jnp.dot(vt_ref[c * sub], pb[:MLA_KV_CHUNK], preferred_element_type=F32)
        for j in range(1, sub):
            pv = pv + jnp.dot(vt_ref[c * sub + j], pb[j * MLA_KV_CHUNK:(j + 1) * MLA_KV_CHUNK],
                              preferred_element_type=F32)
        acc_ref[...] = alpha * acc_ref[...] + pv
        return m_new, l_new

    def step(it, carry):
        m, l = carry
        base = MLA_UNROLL * it
        for u in range(MLA_UNROLL):
            nxt = base + u + 1
            if u == MLA_UNROLL - 1:
                nxt = jnp.minimum(nxt, n_chunks - 1)
            scores(nxt, (u + 1) % MLA_UNROLL)
            m, l = absorb(base + u, u, m, l)
        return m, l

    scores(0, 0)
    init = (jnp.full((1, tq), -jnp.inf, F32), jnp.zeros((1, tq), F32))
    _, l = lax.fori_loop(0, n_chunks // MLA_UNROLL, step, init)
    o_ref[...] = (acc_ref[...] / l).T.astype(o_ref.dtype)


def _mla(qt, k, vt, n_seq, seq, *, tq=512, tk=512):
    t = n_seq * seq
    nq = seq // tq
    assert seq % tq == 0 and seq % (MLA_UNROLL * tk) == 0
    return pl.pallas_call(
        functools.partial(_mla_body, tk=tk),
        grid=(n_seq, MLA_HEADS, nq),
        in_specs=[
            pl.BlockSpec((None, MLA_QK, tq), lambda b, h, i: (h, 0, b * nq + i)),
            pl.BlockSpec((None, seq, MLA_QK), lambda b, h, i: (h, b, 0)),
            pl.BlockSpec((None, seq // MLA_KV_CHUNK, V_HEAD, MLA_KV_CHUNK), lambda b, h, i: (h, b, 0, 0)),
        ],
        out_specs=pl.BlockSpec((tq, V_HEAD), lambda b, h, i: (b * nq + i, h)),
        out_shape=jax.ShapeDtypeStruct((t, MLA_HEADS * V_HEAD), BF16),
        scratch_shapes=[pltpu.VMEM((V_HEAD, tq), F32), pltpu.VMEM((MLA_UNROLL, tk, tq), F32)],
        compiler_params=_params("parallel", "parallel", "arbitrary"),
        name="mla_attention",
    )(qt, k, vt)


def _even_out_body(x_ref, a_ref, u_ref, vl_ref, ws_ref, bs_ref, wo_ref, g_ref, o_ref, b_ref):
    tm = x_ref.shape[0]
    for c in range(tm // GM_CHUNK):
        rows = slice(c * GM_CHUNK, (c + 1) * GM_CHUNK)
        for grp in range(GM_GROUPS):
            cols = slice(grp * 128, (grp + 1) * 128)
            mixed = jnp.dot(ws_ref[grp], vl_ref[rows, cols], preferred_element_type=F32)
            mixed = mixed + bs_ref[:, grp:grp + 1]
            b_ref[rows, cols] = (u_ref[rows, cols].astype(F32) * mixed).astype(BF16)
    n_a = a_ref.shape[1]
    y = jnp.dot(a_ref[...], wo_ref[:n_a, :], preferred_element_type=F32)
    y = y + jnp.dot(b_ref[...], wo_ref[n_a:, :], preferred_element_type=F32)
    o_ref[...] = x_ref[...] + _rms(y, g_ref[...])


def _even_out(x, a, u, vl, w_s, b_s_t, w_o, g, *, tm=512):
    t, d = x.shape
    tok = lambda i: (i, 0)
    return pl.pallas_call(
        _even_out_body,
        grid=(t // tm,),
        in_specs=[
            pl.BlockSpec((tm, d), tok),
            pl.BlockSpec((tm, a.shape[1]), tok),
            pl.BlockSpec((tm, GM_WIDTH), tok),
            pl.BlockSpec((tm, GM_WIDTH), tok),
            _resident(w_s.shape), _resident(b_s_t.shape), _resident(w_o.shape), _resident(g.shape),
        ],
        out_specs=pl.BlockSpec((tm, d), tok),
        out_shape=jax.ShapeDtypeStruct((t, d), F32),
        scratch_shapes=[pltpu.VMEM((tm, GM_WIDTH), BF16)],
        compiler_params=_params("parallel"),
        name="even_out",
    )(x, a, u, vl, w_s, b_s_t, w_o, g)


def _odd_in_body(x_ref, g_ref, wqt_ref, wk_ref, wvt_ref, qt_ref, k_ref, vt_ref):
    h = _rms(x_ref[...], g_ref[...]).astype(BF16)
    qt = lax.dot_general(wqt_ref[...], h, _NT, preferred_element_type=F32)
    qt_ref[...] = (qt * (HEAD_DIM ** -0.5 * LOG2E)).astype(BF16)
    k_ref[...] = jnp.dot(h, wk_ref[...], preferred_element_type=F32).astype(BF16)
    vt_ref[...] = lax.dot_general(wvt_ref[...], h, _NT, preferred_element_type=F32).astype(BF16)


def _odd_in(x, g, w_q_t, w_k, w_v_t, *, tm=512):
    t, d = x.shape
    nq, nk = w_q_t.shape[0], w_k.shape[1]
    consts = [g, w_q_t, w_k, w_v_t]
    return pl.pallas_call(
        _odd_in_body,
        grid=(t // tm,),
        in_specs=[pl.BlockSpec((tm, d), lambda i: (i, 0))] + [_resident(c.shape) for c in consts],
        out_specs=[pl.BlockSpec((nq, tm), lambda i: (0, i)), pl.BlockSpec((tm, nk), lambda i: (i, 0)),
                   pl.BlockSpec((nk, tm), lambda i: (0, i))],
        out_shape=[jax.ShapeDtypeStruct((nq, t), BF16), jax.ShapeDtypeStruct((t, nk), BF16),
                   jax.ShapeDtypeStruct((nk, t), BF16)],
        compiler_params=_params("parallel"),
        name="odd_in",
    )(x, *consts)


def _window_body(sink_ref, qt_ref, kp_ref, km_ref, kn_ref, vtp_ref, vtm_ref, vtn_ref, o_ref, k_scr, vt_scr,
                 *, seq):
    blk = WINDOW
    tq = qt_ref.shape[1]
    k_scr[:blk] = kp_ref[...]
    k_scr[blk:blk + tq] = km_ref[...]
    k_scr[blk + tq:] = kn_ref[...]
    vt_scr[:, :blk] = vtp_ref[...]
    vt_scr[:, blk:blk + tq] = vtm_ref[...]
    vt_scr[:, blk + tq:] = vtn_ref[...]

    j_idx = lax.broadcasted_iota(jnp.int32, (3 * blk, blk), 0)
    a_idx = lax.broadcasted_iota(jnp.int32, (3 * blk, blk), 1)
    dist = jnp.abs(j_idx - blk - a_idx)
    tile_start = pl.program_id(1) * tq

    for r in range(tq // blk):
        key_pos = tile_start + (r - 1) * blk + j_idx
        valid = (dist <= WINDOW) & (key_pos >= 0) & (key_pos < seq)
        neg_dist = jnp.where(valid, dist.astype(F32) * -LOG2E, -jnp.inf)
        q_lanes = slice(r * blk, (r + 1) * blk)
        win = slice(r * blk, (r + 3) * blk)
        for kvh in range(GQA_KV_HEADS):
            feat = slice(kvh * HEAD_DIM, (kvh + 1) * HEAD_DIM)
            heads = [kvh * GQA_GROUP + grp for grp in range(GQA_GROUP)]
            qt = jnp.concatenate([qt_ref[hd * HEAD_DIM:(hd + 1) * HEAD_DIM, q_lanes] for hd in heads], axis=1)
            st = jnp.dot(k_scr[win, feat], qt, preferred_element_type=F32)
            bias = jnp.concatenate([2.0 ** (-8.0 * (hd + 1) / GQA_HEADS) * neg_dist for hd in heads], axis=1)
            sink = jnp.concatenate([jnp.full((1, blk), sink_ref[hd] * LOG2E, F32) for hd in heads], axis=1)
            logits = st + bias
            m = jnp.maximum(jnp.max(logits, axis=0, keepdims=True), sink)
            e = jnp.exp2(logits - m)
            denom = jnp.sum(e, axis=0, keepdims=True) + jnp.exp2(sink - m)
            ot = jnp.dot(vt_scr[feat, win], e.astype(BF16), preferred_element_type=F32) / denom
            for grp, hd in enumerate(heads):
                o_ref[q_lanes, hd * HEAD_DIM:(hd + 1) * HEAD_DIM] = (
                    ot[:, grp * blk:(grp + 1) * blk].T.astype(o_ref.dtype))


def _window(qt, k, vt, sink, n_seq, seq, *, tq=512):
    t = n_seq * seq
    nq = seq // tq
    r = tq // WINDOW
    nk = k.shape[1]
    blocks_per_seq = seq // WINDOW
    main = lambda b, i: b * nq + i
    prev = lambda b, i: b * blocks_per_seq + jnp.maximum(i * r - 1, 0)
    nxt = lambda b, i: b * blocks_per_seq + jnp.minimum((i + 1) * r, blocks_per_seq - 1)
    rows = lambda size, idx: pl.BlockSpec((size, nk), lambda b, i: (idx(b, i), 0))
    lanes = lambda size, idx: pl.BlockSpec((nk, size), lambda b, i: (0, idx(b, i)))
    return pl.pallas_call(
        functools.partial(_window_body, seq=seq),
        grid=(n_seq, nq),
        in_specs=[pl.BlockSpec(memory_space=pltpu.SMEM),
                  pl.BlockSpec((qt.shape[0], tq), lambda b, i: (0, main(b, i))),
                  rows(WINDOW, prev), rows(tq, main), rows(WINDOW, nxt),
                  lanes(WINDOW, prev), lanes(tq, main), lanes(WINDOW, nxt)],
        out_specs=pl.BlockSpec((tq, qt.shape[0]), lambda b, i: (main(b, i), 0)),
        out_shape=jax.ShapeDtypeStruct((t, qt.shape[0]), BF16),
        scratch_shapes=[pltpu.VMEM((tq + 2 * WINDOW, nk), BF16), pltpu.VMEM((nk, tq + 2 * WINDOW), BF16)],
        compiler_params=_params("parallel", "parallel"),
        name="window_attention",
    )(sink, qt, k, k, k, vt, vt, vt)


def _odd_out_body(x_ref, a_ref, wo_ref, g_ref, o_ref):
    y = jnp.dot(a_ref[...], wo_ref[...], preferred_element_type=F32)
    o_ref[...] = x_ref[...] + _rms(y, g_ref[...])


def _odd_out(x, a, w_o, g, *, tm=512):
    t, d = x.shape
    tok = lambda i: (i, 0)
    return pl.pallas_call(
        _odd_out_body,
        grid=(t // tm,),
        in_specs=[pl.BlockSpec((tm, d), tok), pl.BlockSpec((tm, a.shape[1]), tok),
                  _resident(w_o.shape), _resident(g.shape)],
        out_specs=pl.BlockSpec((tm, d), tok),
        out_shape=jax.ShapeDtypeStruct((t, d), F32),
        compiler_params=_params("parallel"),
        name="odd_out",
    )(x, a, w_o, g)


def _swap_halves(w):
    half = w.shape[-1] // 2
    return jnp.concatenate([-w[..., half:], w[..., :half]], axis=-1)


def _prep_even(w_in, w_uq, w_ukv):
    o_u = Q_LORA + KV_LORA
    o_g = o_u + QK_ROPE
    k_r = w_in[:, o_u:o_g]
    w_in_x = jnp.concatenate([w_in[:, :o_u], w_in[:, o_g:], k_r, _swap_halves(k_r)], axis=-1)
    per_head = w_uq.reshape(w_uq.shape[0], MLA_HEADS, QK_NOPE + QK_ROPE)
    rope = per_head[..., QK_NOPE:]
    w_uq_x = jnp.concatenate([per_head, _swap_halves(rope)], axis=-1).reshape(w_uq.shape[0], MLA_HEADS * MLA_QK)
    kv = w_ukv.reshape(w_ukv.shape[0], MLA_HEADS, QK_NOPE + V_HEAD)
    w_uk = kv[..., :QK_NOPE].reshape(w_ukv.shape[0], MLA_HEADS * QK_NOPE)
    w_uv = kv[..., QK_NOPE:].reshape(w_ukv.shape[0], MLA_HEADS * V_HEAD)
    return w_in_x.astype(BF16), w_uq_x.T.astype(BF16), w_uk.astype(BF16), w_uv.T.astype(BF16)


def _prep_odd(w_qkv):
    nq = GQA_HEADS * HEAD_DIM
    nk = GQA_KV_HEADS * HEAD_DIM
    return (w_qkv[:, :nq].T.astype(BF16), w_qkv[:, nq:nq + nk].astype(BF16), w_qkv[:, nq + nk:].T.astype(BF16))


def _rope_table(seq):
    pos = jnp.arange(seq, dtype=F32)
    inv = ROPE_THETA ** (-jnp.arange(0, QK_ROPE, 2, dtype=F32) / QK_ROPE)
    ang = pos[:, None] * inv[None, :]
    cos, sin = jnp.cos(ang), jnp.sin(ang)
    return jnp.concatenate([cos, cos, sin, sin], axis=-1)


def _trunk(x3, w):
    n_seq, seq, d = x3.shape
    x = x3.reshape(n_seq * seq, d)
    depth = w["norm_g"].shape[0]
    for layer in range(depth):
        g = w["norm_g"][layer]
        row = lambda i: g[i][None, :]
        ffn = lambda x, n, idx: _ffn(x, row(n), row(n + 1), w["gate_up"], w["down"], layer, idx)
        x = ffn(x, 0, 0)
        i = layer // 2
        if layer % 2 == 0:
            w_in, w_uq_t, w_uk, w_uv_t = w["even"][i]
            qt, k, vt, u, vl = _even_in(x, row(2), w_in, w["g_q"][i][None, :], w["g_kv"][i][None, :],
                                        w_uq_t, w_uk, w_uv_t, w["g_v"][i][None, :], _rope_table(seq), seq)
            a = _mla(qt, k, vt, n_seq, seq)
            x = _even_out(x, a, u, vl, w["w_s"][i], w["b_s_t"][i], w["even_o"][i], row(3))
        else:
            qt, k, vt = _odd_in(x, row(2), *w["odd"][i])
            a = _window(qt, k, vt, w["sink"][i], n_seq, seq)
            x = _odd_out(x, a, w["odd_o"][i], row(3))
        x = ffn(x, 4, 1)
    return x.reshape(n_seq, seq, d)


def kernel(x_prompt, x_sample, norm_g, ffn_w_gate, ffn_w_up, ffn_w_down, even_w_in, mla_g_q, mla_g_kv, mla_w_uq, mla_w_ukv, gm_g_v, gm_w_s, gm_b_s, even_w_o, odd_w_qkv, odd_sink, odd_w_o):
    even = [_prep_even(even_w_in[i], mla_w_uq[i], mla_w_ukv[i]) for i in range(even_w_in.shape[0])]
    w = {
        "norm_g": norm_g,
        "gate_up": _prep_ffn(ffn_w_gate, ffn_w_up, FFN_CHUNK), "down": ffn_w_down.astype(BF16),
        "even": even, "g_q": mla_g_q, "g_kv": mla_g_kv, "g_v": gm_g_v,
        "w_s": gm_w_s.astype(BF16), "b_s_t": jnp.swapaxes(gm_b_s, 1, 2), "even_o": even_w_o.astype(BF16),
        "odd": [_prep_odd(odd_w_qkv[i]) for i in range(odd_w_qkv.shape[0])],
        "sink": odd_sink, "odd_o": odd_w_o.astype(BF16),
    }
    return (_trunk(x_prompt, w), _trunk(x_sample, w))
```

```python
import functools

import jax
import jax.numpy as jnp
from jax import lax
from jax.experimental import pallas as pl
from jax.experimental.pallas import tpu as pltpu

F32 = jnp.float32
BF16 = jnp.bfloat16

EPS = 1e-6
LOG2E = 1.4426950408889634

MLA_HEADS = 8
Q_LORA = 768
KV_LORA = 512
QK_NOPE = 128
QK_ROPE = 64
V_HEAD = 128
ROPE_THETA = 10000.0
MLA_QK = QK_NOPE + 2 * QK_ROPE
MLA_KV_CHUNK = 256
MLA_UNROLL = 16
MLA_SLOTS = 4
MLA_V_ROWS = V_HEAD + 16

GM_GROUPS = 8
GM_CHUNK = 128
GM_WIDTH = GM_GROUPS * 128

GQA_HEADS = 16
GQA_KV_HEADS = 4
GQA_GROUP = GQA_HEADS // GQA_KV_HEADS
HEAD_DIM = 128
WINDOW = 128

FFN_ROW_SPLIT = 2
FFN_CHUNK = 512

VMEM_LIMIT_BYTES = 56 * 1024 * 1024


def _params(*semantics):
    return pltpu.CompilerParams(dimension_semantics=semantics, vmem_limit_bytes=VMEM_LIMIT_BYTES)


def _resident(shape):
    zeros = (0,) * len(shape)
    return pl.BlockSpec(shape, lambda *_: zeros, pipeline_mode=pl.Buffered(1))


def _rms(x, g):
    return x * lax.rsqrt(jnp.mean(x * x, axis=-1, keepdims=True) + EPS) * g


def _ffn_body(x_ref, gin_ref, gout_ref, wg_ref, wu_ref, wd_ref, o_ref, xn_ref, acc_ref):
    j = pl.program_id(1)
    last = pl.num_programs(1) - 1
    tr = x_ref.shape[0] // FFN_ROW_SPLIT
    row_blocks = [slice(r * tr, (r + 1) * tr) for r in range(FFN_ROW_SPLIT)]

    def chunk(rows, first):
        xn = xn_ref[rows, :]
        gate = jnp.dot(xn, wg_ref[...], preferred_element_type=F32)
        up = jnp.dot(xn, wu_ref[...], preferred_element_type=F32)
        h = (gate * jax.nn.sigmoid(gate) * up).astype(BF16)
        part = jnp.dot(h, wd_ref[...], preferred_element_type=F32)
        if first:
            acc_ref[rows, :] = part
        else:
            acc_ref[rows, :] += part

    @pl.when(j == 0)
    def _():
        for rows in row_blocks:
            xn_ref[rows, :] = _rms(x_ref[rows, :], gin_ref[...]).astype(BF16)
            chunk(rows, True)

    @pl.when((j > 0) & (j < last))
    def _():
        for rows in row_blocks:
            chunk(rows, False)

    @pl.when(j == last)
    def _():
        for rows in row_blocks:
            chunk(rows, False)
            o_ref[rows, :] = x_ref[rows, :] + 0.5 * _rms(acc_ref[rows, :], gout_ref[...])


def _ffn(x, g_in, g_out, w_gate, w_up, w_down, layer, idx, *, tm=512, tf=FFN_CHUNK):
    t, d = x.shape
    n_chunks = w_gate.shape[-1] // tf
    assert n_chunks >= 3, "first / middle / last d_ff chunks are separate code paths"
    return pl.pallas_call(
        _ffn_body,
        grid=(t // tm, n_chunks),
        in_specs=[
            pl.BlockSpec((tm, d), lambda i, j: (i, 0)),
            pl.BlockSpec((1, d), lambda i, j: (0, 0)),
            pl.BlockSpec((1, d), lambda i, j: (0, 0)),
            pl.BlockSpec((None, None, d, tf), lambda i, j: (layer, idx, 0, j)),
            pl.BlockSpec((None, None, d, tf), lambda i, j: (layer, idx, 0, j)),
            pl.BlockSpec((None, None, tf, d), lambda i, j: (layer, idx, j, 0)),
        ],
        out_specs=pl.BlockSpec((tm, d), lambda i, j: (i, 0)),
        out_shape=jax.ShapeDtypeStruct((t, d), F32),
        scratch_shapes=[pltpu.VMEM((tm, d), BF16), pltpu.VMEM((tm, d), F32)],
        compiler_params=_params("parallel", "arbitrary"),
        name="ffn",
    )(x, g_in, g_out, w_gate, w_up, w_down)


_NT = (((1,), (1,)), ((), ()))


def _even_in_body(x_ref, g_ref, win_ref, gq_ref, gkv_ref, wuqt_ref, wuk_ref, wuvt_ref, gv_ref, tab_ref,
                  tabt_ref, qt_ref, k_ref, vt_ref, u_ref, vl_ref):
    h = _rms(x_ref[...], g_ref[...]).astype(BF16)
    p = jnp.dot(h, win_ref[...], preferred_element_type=F32)
    o_kv = Q_LORA
    o_u = Q_LORA + KV_LORA
    o_v = o_u + GM_WIDTH
    o_r = o_v + GM_WIDTH

    c_q = _rms(p[:, :o_kv], gq_ref[...]).astype(BF16)
    c_kv = _rms(p[:, o_kv:o_u], gkv_ref[...]).astype(BF16)
    qt = lax.dot_general(wuqt_ref[...], c_q, _NT, preferred_element_type=F32)
    vt = lax.dot_general(wuvt_ref[...], c_kv, _NT, preferred_element_type=F32)
    k_nope = jnp.dot(c_kv, wuk_ref[...], preferred_element_type=F32)

    a = p[:, o_r:o_r + 2 * QK_ROPE] * tab_ref[...]
    k_rope = (a + pltpu.roll(a, QK_ROPE, axis=1)).astype(BF16)
    tabt = tabt_ref[...]
    q_scale = (QK_NOPE + QK_ROPE) ** -0.5 * LOG2E
    tm = x_ref.shape[0]
    row = lax.broadcasted_iota(jnp.int32, (MLA_V_ROWS - V_HEAD, MLA_KV_CHUNK), 0)
    ones_rows = jnp.where(row == 0, 1.0, 0.0).astype(BF16)
    for hd in range(MLA_HEADS):
        qo = hd * MLA_QK
        qt_ref[hd, :QK_NOPE, :] = (qt[qo:qo + QK_NOPE] * q_scale).astype(BF16)
        ar = qt[qo + QK_NOPE:qo + MLA_QK] * tabt
        qt_ref[hd, QK_NOPE:QK_NOPE + QK_ROPE, :] = ((ar[:QK_ROPE] + ar[QK_ROPE:]) * q_scale).astype(BF16)
        qt_ref[hd, QK_NOPE + QK_ROPE:, :] = jnp.zeros((QK_ROPE, tm), BF16)
        k_ref[hd, :, :QK_NOPE] = k_nope[:, hd * QK_NOPE:(hd + 1) * QK_NOPE].astype(BF16)
        k_ref[hd, :, QK_NOPE:] = k_rope
        for c in range(tm // MLA_KV_CHUNK):
            vt_ref[hd, c, :V_HEAD, :] = vt[hd * V_HEAD:(hd + 1) * V_HEAD,
                                           c * MLA_KV_CHUNK:(c + 1) * MLA_KV_CHUNK].astype(BF16)
            vt_ref[hd, c, V_HEAD:, :] = ones_rows

    u_ref[...] = jax.nn.gelu(p[:, o_u:o_v]).astype(BF16)
    gv = jax.nn.gelu(p[:, o_v:o_r])
    gc = gv - jnp.mean(gv, axis=-1, keepdims=True)
    ln = gc * lax.rsqrt(jnp.mean(gc * gc, axis=-1, keepdims=True) + EPS) * gv_ref[...]
    vl_ref[...] = ln.astype(BF16)


def _even_in(x, g, w_in, g_q, g_kv, w_uq_t, w_uk, w_uv_t, g_v, tab, seq, *, tm=256):
    t, d = x.shape
    tiles_per_seq = seq // tm
    tok = lambda i: (i, 0)
    consts = [g, w_in, g_q, g_kv, w_uq_t, w_uk, w_uv_t, g_v]
    return pl.pallas_call(
        _even_in_body,
        grid=(t // tm,),
        in_specs=[pl.BlockSpec((tm, d), tok)] + [_resident(c.shape) for c in consts] + [
            pl.BlockSpec((tm, 2 * QK_ROPE), lambda i: (i % tiles_per_seq, 0)),
            pl.BlockSpec((2 * QK_ROPE, tm), lambda i: (0, i % tiles_per_seq)),
        ],
        out_specs=[
            pl.BlockSpec((MLA_HEADS, MLA_QK, tm), lambda i: (0, 0, i)),
            pl.BlockSpec((MLA_HEADS, tm, MLA_QK), lambda i: (0, i, 0)),
            pl.BlockSpec((MLA_HEADS, tm // MLA_KV_CHUNK, MLA_V_ROWS, MLA_KV_CHUNK), lambda i: (0, i, 0, 0)),
            pl.BlockSpec((tm, GM_WIDTH), tok),
            pl.BlockSpec((tm, GM_WIDTH), tok),
        ],
        out_shape=[
            jax.ShapeDtypeStruct((MLA_HEADS, MLA_QK, t), BF16),
            jax.ShapeDtypeStruct((MLA_HEADS, t, MLA_QK), BF16),
            jax.ShapeDtypeStruct((MLA_HEADS, t // MLA_KV_CHUNK, MLA_V_ROWS, MLA_KV_CHUNK), BF16),
            jax.ShapeDtypeStruct((t, GM_WIDTH), BF16),
            jax.ShapeDtypeStruct((t, GM_WIDTH), BF16),
        ],
        compiler_params=_params("parallel"),
        name="even_in",
    )(x, *consts, tab, tab.T)


def _mla_body(qt_ref, k_ref, vt_ref, o_ref, acc_ref, s_ref, *, tk):
    qt = qt_ref[...]
    tq = qt.shape[1]
    sub = tk // MLA_KV_CHUNK
    n_chunks = k_ref.shape[0] // tk
    unroll = min(MLA_UNROLL, max(n_chunks // 2, MLA_SLOTS))
    assert n_chunks % unroll == 0 and unroll % MLA_SLOTS == 0
    acc_ref[...] = jnp.zeros(acc_ref.shape, F32)

    def scores(c, slot):
        off = pl.multiple_of(c * tk, tk)
        st = jnp.dot(k_ref[pl.ds(off, tk), :], qt, preferred_element_type=F32)
        s_ref[slot] = st
        return jnp.max(st, axis=0, keepdims=True)

    def absorb(c, slot, m_chunk, m_old):
        m_new = jnp.maximum(m_old, m_chunk)
        alpha = jnp.exp2(m_old - m_new)
        pb = jnp.exp2(s_ref[slot] - m_new).astype(BF16)
        pv = jnp.dot(vt_ref[c * sub], pb[:MLA_KV_CHUNK], preferred_element_type=F32)
        for j in range(1, sub):
            pv = pv + jnp.dot(vt_ref[c * sub + j], pb[j * MLA_KV_CHUNK:(j + 1) * MLA_KV_CHUNK],
                              preferred_element_type=F32)
        acc_ref[...] = alpha * acc_ref[...] + pv
        return m_new

    def step(it, carry):
        m, m_chunk = carry
        base = unroll * it
        for u in range(unroll):
            nxt = base + u + 1
            if u == unroll - 1:
                nxt = jnp.minimum(nxt, n_chunks - 1)
            m_next = scores(nxt, (u + 1) % MLA_SLOTS)
            m = absorb(base + u, u % MLA_SLOTS, m_chunk, m)
            m_chunk = m_next
        return m, m_chunk

    lax.fori_loop(0, n_chunks // unroll, step, (jnp.full((1, tq), -jnp.inf, F32), scores(0, 0)))
    o_ref[...] = (acc_ref[:V_HEAD, :] / acc_ref[V_HEAD:V_HEAD + 1, :]).T.astype(o_ref.dtype)


def _mla(qt, k, vt, n_seq, seq, *, tq=1024, tk=512):
    t = n_seq * seq
    nq = seq // tq
    assert seq % tq == 0 and seq % tk == 0
    return pl.pallas_call(
        functools.partial(_mla_body, tk=tk),
        grid=(n_seq, MLA_HEADS, nq),
        in_specs=[
            pl.BlockSpec((None, MLA_QK, tq), lambda b, h, i: (h, 0, b * nq + i)),
            pl.BlockSpec((None, seq, MLA_QK), lambda b, h, i: (h, b, 0)),
            pl.BlockSpec((None, seq // MLA_KV_CHUNK, MLA_V_ROWS, MLA_KV_CHUNK), lambda b, h, i: (h, b, 0, 0)),
        ],
        out_specs=pl.BlockSpec((tq, V_HEAD), lambda b, h, i: (b * nq + i, h)),
        out_shape=jax.ShapeDtypeStruct((t, MLA_HEADS * V_HEAD), BF16),
        scratch_shapes=[pltpu.VMEM((MLA_V_ROWS, tq), F32), pltpu.VMEM((MLA_SLOTS, tk, tq), F32)],
        compiler_params=_params("parallel", "parallel", "arbitrary"),
        name="mla_attention",
    )(qt, k, vt)


def _even_out_body(x_ref, a_ref, u_ref, vl_ref, ws_ref, bs_ref, wo_ref, g_ref, o_ref, b_ref):
    tm = x_ref.shape[0]
    for c in range(tm // GM_CHUNK):
        rows = slice(c * GM_CHUNK, (c + 1) * GM_CHUNK)
        for grp in range(GM_GROUPS):
            cols = slice(grp * 128, (grp + 1) * 128)
            mixed = jnp.dot(ws_ref[grp], vl_ref[rows, cols], preferred_element_type=F32)
            mixed = mixed + bs_ref[:, grp:grp + 1]
            b_ref[rows, cols] = (u_ref[rows, cols].astype(F32) * mixed).astype(BF16)
    n_a = a_ref.shape[1]
    y = jnp.dot(a_ref[...], wo_ref[:n_a, :], preferred_element_type=F32)
    y = y + jnp.dot(b_ref[...], wo_ref[n_a:, :], preferred_element_type=F32)
    o_ref[...] = x_ref[...] + _rms(y, g_ref[...])


def _even_out(x, a, u, vl, w_s, b_s_t, w_o, g, *, tm=512):
    t, d = x.shape
    tok = lambda i: (i, 0)
    return pl.pallas_call(
        _even_out_body,
        grid=(t // tm,),
        in_specs=[
            pl.BlockSpec((tm, d), tok),
            pl.BlockSpec((tm, a.shape[1]), tok),
            pl.BlockSpec((tm, GM_WIDTH), tok),
            pl.BlockSpec((tm, GM_WIDTH), tok),
            _resident(w_s.shape), _resident(b_s_t.shape), _resident(w_o.shape), _resident(g.shape),
        ],
        out_specs=pl.BlockSpec((tm, d), tok),
        out_shape=jax.ShapeDtypeStruct((t, d), F32),
        scratch_shapes=[pltpu.VMEM((tm, GM_WIDTH), BF16)],
        compiler_params=_params("parallel"),
        name="even_out",
    )(x, a, u, vl, w_s, b_s_t, w_o, g)


def _odd_in_body(x_ref, g_ref, wqt_ref, wk_ref, wvt_ref, qt_ref, k_ref, vt_ref):
    h = _rms(x_ref[...], g_ref[...]).astype(BF16)
    qt = lax.dot_general(wqt_ref[...], h, _NT, preferred_element_type=F32)
    qt_ref[...] = (qt * (HEAD_DIM ** -0.5 * LOG2E)).astype(BF16)
    k_ref[...] = jnp.dot(h, wk_ref[...], preferred_element_type=F32).astype(BF16)
    vt_ref[...] = lax.dot_general(wvt_ref[...], h, _NT, preferred_element_type=F32).astype(BF16)


def _odd_in(x, g, w_q_t, w_k, w_v_t, *, tm=512):
    t, d = x.shape
    nq, nk = w_q_t.shape[0], w_k.shape[1]
    consts = [g, w_q_t, w_k, w_v_t]
    return pl.pallas_call(
        _odd_in_body,
        grid=(t // tm,),
        in_specs=[pl.BlockSpec((tm, d), lambda i: (i, 0))] + [_resident(c.shape) for c in consts],
        out_specs=[pl.BlockSpec((nq, tm), lambda i: (0, i)), pl.BlockSpec((tm, nk), lambda i: (i, 0)),
                   pl.BlockSpec((nk, tm), lambda i: (0, i))],
        out_shape=[jax.ShapeDtypeStruct((nq, t), BF16), jax.ShapeDtypeStruct((t, nk), BF16),
                   jax.ShapeDtypeStruct((nk, t), BF16)],
        compiler_params=_params("parallel"),
        name="odd_in",
    )(x, *consts)


def _window_body(sink_ref, qt_ref, kp_ref, km_ref, kn_ref, vtp_ref, vtm_ref, vtn_ref, o_ref, k_scr, vt_scr,
                 *, seq):
    blk = WINDOW
    tq = qt_ref.shape[1]
    k_scr[:blk] = kp_ref[...]
    k_scr[blk:blk + tq] = km_ref[...]
    k_scr[blk + tq:] = kn_ref[...]
    vt_scr[:, :blk] = vtp_ref[...]
    vt_scr[:, blk:blk + tq] = vtm_ref[...]
    vt_scr[:, blk + tq:] = vtn_ref[...]

    j_idx = lax.broadcasted_iota(jnp.int32, (3 * blk, blk), 0)
    a_idx = lax.broadcasted_iota(jnp.int32, (3 * blk, blk), 1)
    dist = jnp.abs(j_idx - blk - a_idx)
    tile_start = pl.program_id(1) * tq

    for r in range(tq // blk):
        key_pos = tile_start + (r - 1) * blk + j_idx
        valid = (dist <= WINDOW) & (key_pos >= 0) & (key_pos < seq)
        neg_dist = jnp.where(valid, dist.astype(F32) * -LOG2E, -jnp.inf)
        q_lanes = slice(r * blk, (r + 1) * blk)
        win = slice(r * blk, (r + 3) * blk)
        for kvh in range(GQA_KV_HEADS):
            feat = slice(kvh * HEAD_DIM, (kvh + 1) * HEAD_DIM)
            heads = [kvh * GQA_GROUP + grp for grp in range(GQA_GROUP)]
            qt = jnp.concatenate([qt_ref[hd * HEAD_DIM:(hd + 1) * HEAD_DIM, q_lanes] for hd in heads], axis=1)
            st = jnp.dot(k_scr[win, feat], qt, preferred_element_type=F32)
            bias = jnp.concatenate([2.0 ** (-8.0 * (hd + 1) / GQA_HEADS) * neg_dist for hd in heads], axis=1)
            sink = jnp.concatenate([jnp.full((1, blk), sink_ref[hd] * LOG2E, F32) for hd in heads], axis=1)
            logits = st + bias
            m = jnp.maximum(jnp.max(logits, axis=0, keepdims=True), sink)
            e = jnp.exp2(logits - m)
            denom = jnp.sum(e, axis=0, keepdims=True) + jnp.exp2(sink - m)
            ot = jnp.dot(vt_scr[feat, win], e.astype(BF16), preferred_element_type=F32) / denom
            for grp, hd in enumerate(heads):
                o_ref[q_lanes, hd * HEAD_DIM:(hd + 1) * HEAD_DIM] = (
                    ot[:, grp * blk:(grp + 1) * blk].T.astype(o_ref.dtype))


def _window(qt, k, vt, sink, n_seq, seq, *, tq=512):
    t = n_seq * seq
    nq = seq // tq
    r = tq // WINDOW
    nk = k.shape[1]
    blocks_per_seq = seq // WINDOW
    main = lambda b, i: b * nq + i
    prev = lambda b, i: b * blocks_per_seq + jnp.maximum(i * r - 1, 0)
    nxt = lambda b, i: b * blocks_per_seq + jnp.minimum((i + 1) * r, blocks_per_seq - 1)
    rows = lambda size, idx: pl.BlockSpec((size, nk), lambda b, i: (idx(b, i), 0))
    lanes = lambda size, idx: pl.BlockSpec((nk, size), lambda b, i: (0, idx(b, i)))
    return pl.pallas_call(
        functools.partial(_window_body, seq=seq),
        grid=(n_seq, nq),
        in_specs=[pl.BlockSpec(memory_space=pltpu.SMEM),
                  pl.BlockSpec((qt.shape[0], tq), lambda b, i: (0, main(b, i))),
                  rows(WINDOW, prev), rows(tq, main), rows(WINDOW, nxt),
                  lanes(WINDOW, prev), lanes(tq, main), lanes(WINDOW, nxt)],
        out_specs=pl.BlockSpec((tq, qt.shape[0]), lambda b, i: (main(b, i), 0)),
        out_shape=jax.ShapeDtypeStruct((t, qt.shape[0]), BF16),
        scratch_shapes=[pltpu.VMEM((tq + 2 * WINDOW, nk), BF16), pltpu.VMEM((nk, tq + 2 * WINDOW), BF16)],
        compiler_params=_params("parallel", "parallel"),
        name="window_attention",
    )(sink, qt, k, k, k, vt, vt, vt)


def _odd_out_body(x_ref, a_ref, wo_ref, g_ref, o_ref):
    y = jnp.dot(a_ref[...], wo_ref[...], preferred_element_type=F32)
    o_ref[...] = x_ref[...] + _rms(y, g_ref[...])


def _odd_out(x, a, w_o, g, *, tm=512):
    t, d = x.shape
    tok = lambda i: (i, 0)
    return pl.pallas_call(
        _odd_out_body,
        grid=(t // tm,),
        in_specs=[pl.BlockSpec((tm, d), tok), pl.BlockSpec((tm, a.shape[1]), tok),
                  _resident(w_o.shape), _resident(g.shape)],
        out_specs=pl.BlockSpec((tm, d), tok),
        out_shape=jax.ShapeDtypeStruct((t, d), F32),
        compiler_params=_params("parallel"),
        name="odd_out",
    )(x, a, w_o, g)


def _swap_halves(w):
    half = w.shape[-1] // 2
    return jnp.concatenate([-w[..., half:], w[..., :half]], axis=-1)


def _prep_even(w_in, w_uq, w_ukv):
    o_u = Q_LORA + KV_LORA
    o_g = o_u + QK_ROPE
    k_r = w_in[:, o_u:o_g]
    w_in_x = jnp.concatenate([w_in[:, :o_u], w_in[:, o_g:], k_r, _swap_halves(k_r)], axis=-1)
    per_head = w_uq.reshape(w_uq.shape[0], MLA_HEADS, QK_NOPE + QK_ROPE)
    rope = per_head[..., QK_NOPE:]
    w_uq_x = jnp.concatenate([per_head, _swap_halves(rope)], axis=-1).reshape(w_uq.shape[0], MLA_HEADS * MLA_QK)
    kv = w_ukv.reshape(w_ukv.shape[0], MLA_HEADS, QK_NOPE + V_HEAD)
    w_uk = kv[..., :QK_NOPE].reshape(w_ukv.shape[0], MLA_HEADS * QK_NOPE)
    w_uv = kv[..., QK_NOPE:].reshape(w_ukv.shape[0], MLA_HEADS * V_HEAD)
    return w_in_x.astype(BF16), w_uq_x.T.astype(BF16), w_uk.astype(BF16), w_uv.T.astype(BF16)


def _prep_odd(w_qkv):
    nq = GQA_HEADS * HEAD_DIM
    nk = GQA_KV_HEADS * HEAD_DIM
    return (w_qkv[:, :nq].T.astype(BF16), w_qkv[:, nq:nq + nk].astype(BF16), w_qkv[:, nq + nk:].T.astype(BF16))


def _rope_table(seq):
    pos = jnp.arange(seq, dtype=F32)
    inv = ROPE_THETA ** (-jnp.arange(0, QK_ROPE, 2, dtype=F32) / QK_ROPE)
    ang = pos[:, None] * inv[None, :]
    cos, sin = jnp.cos(ang), jnp.sin(ang)
    return jnp.concatenate([cos, cos, sin, sin], axis=-1)


def _trunk(x3, w):
    n_seq, seq, d = x3.shape
    x = x3.reshape(n_seq * seq, d)
    depth = w["norm_g"].shape[0]
    for layer in range(depth):
        g = w["norm_g"][layer]
        row = lambda i: g[i][None, :]
        ffn = lambda x, n, idx: _ffn(x, row(n), row(n + 1), w["gate"], w["up"], w["down"], layer, idx)
        x = ffn(x, 0, 0)
        i = layer // 2
        if layer % 2 == 0:
            w_in, w_uq_t, w_uk, w_uv_t = w["even"][i]
            qt, k, vt, u, vl = _even_in(x, row(2), w_in, w["g_q"][i][None, :], w["g_kv"][i][None, :],
                                        w_uq_t, w_uk, w_uv_t, w["g_v"][i][None, :], _rope_table(seq), seq)
            a = _mla(qt, k, vt, n_seq, seq)
            x = _even_out(x, a, u, vl, w["w_s"][i], w["b_s_t"][i], w["even_o"][i], row(3))
        else:
            qt, k, vt = _odd_in(x, row(2), *w["odd"][i])
            a = _window(qt, k, vt, w["sink"][i], n_seq, seq)
            x = _odd_out(x, a, w["odd_o"][i], row(3))
        x = ffn(x, 4, 1)
    return x.reshape(n_seq, seq, d)


def kernel(x_prompt, x_sample, norm_g, ffn_w_gate, ffn_w_up, ffn_w_down, even_w_in, mla_g_q, mla_g_kv, mla_w_uq, mla_w_ukv, gm_g_v, gm_w_s, gm_b_s, even_w_o, odd_w_qkv, odd_sink, odd_w_o):
    even = [_prep_even(even_w_in[i], mla_w_uq[i], mla_w_ukv[i]) for i in range(even_w_in.shape[0])]
    w = {
        "norm_g": norm_g,
        "gate": ffn_w_gate.astype(BF16), "up": ffn_w_up.astype(BF16), "down": ffn_w_down.astype(BF16),
        "even": even, "g_q": mla_g_q, "g_kv": mla_g_kv, "g_v": gm_g_v,
        "w_s": gm_w_s.astype(BF16), "b_s_t": jnp.swapaxes(gm_b_s, 1, 2), "even_o": even_w_o.astype(BF16),
        "odd": [_prep_odd(odd_w_qkv[i]) for i in range(odd_w_qkv.shape[0])],
        "sink": odd_sink, "odd_o": odd_w_o.astype(BF16),
    }
    return (_trunk(x_prompt, w), _trunk(x_sample, w))
```

```python
import functools

import jax
import jax.numpy as jnp
from jax import lax
from jax.experimental import pallas as pl
from jax.experimental.pallas import tpu as pltpu

F32 = jnp.float32
BF16 = jnp.bfloat16

EPS = 1e-6
LOG2E = 1.4426950408889634

MLA_HEADS = 8
Q_LORA = 768
KV_LORA = 512
QK_NOPE = 128
QK_ROPE = 64
V_HEAD = 128
ROPE_THETA = 10000.0
MLA_QK = QK_NOPE + 2 * QK_ROPE
MLA_KV_CHUNK = 256
MLA_TQ = 1024
MLA_Q_TILES = 4
MLA_UNROLL = 16
MLA_SLOTS = 4
MLA_V_ROWS = V_HEAD + 16

GM_GROUPS = 8
GM_CHUNK = 128
GM_WIDTH = GM_GROUPS * 128

GQA_HEADS = 16
GQA_KV_HEADS = 4
GQA_GROUP = GQA_HEADS // GQA_KV_HEADS
HEAD_DIM = 128
WINDOW = 128

FFN_ROW_SPLIT = 2
FFN_CHUNK = 512

VMEM_LIMIT_BYTES = 56 * 1024 * 1024


def _params(*semantics):
    return pltpu.CompilerParams(dimension_semantics=semantics, vmem_limit_bytes=VMEM_LIMIT_BYTES)


def _resident(shape):
    zeros = (0,) * len(shape)
    return pl.BlockSpec(shape, lambda *_: zeros, pipeline_mode=pl.Buffered(1))


def _rms(x, g):
    return x * lax.rsqrt(jnp.mean(x * x, axis=-1, keepdims=True) + EPS) * g


def _ffn_body(x_ref, gin_ref, gout_ref, wg_ref, wu_ref, wd_ref, o_ref, xn_ref):
    acc_ref = o_ref
    j = pl.program_id(1)
    last = pl.num_programs(1) - 1
    tr = x_ref.shape[0] // FFN_ROW_SPLIT
    row_blocks = [slice(r * tr, (r + 1) * tr) for r in range(FFN_ROW_SPLIT)]

    def chunk(rows, first):
        xn = xn_ref[rows, :]
        gate = jnp.dot(xn, wg_ref[...], preferred_element_type=F32)
        up = jnp.dot(xn, wu_ref[...], preferred_element_type=F32)
        h = (gate * jax.nn.sigmoid(gate) * up).astype(BF16)
        part = jnp.dot(h, wd_ref[...], preferred_element_type=F32)
        if first:
            acc_ref[rows, :] = part
        else:
            acc_ref[rows, :] += part

    @pl.when(j == 0)
    def _():
        for rows in row_blocks:
            xn_ref[rows, :] = _rms(x_ref[rows, :], gin_ref[...]).astype(BF16)
            chunk(rows, True)

    @pl.when((j > 0) & (j < last))
    def _():
        for rows in row_blocks:
            chunk(rows, False)

    @pl.when(j == last)
    def _():
        for rows in row_blocks:
            chunk(rows, False)
            o_ref[rows, :] = x_ref[rows, :] + 0.5 * _rms(acc_ref[rows, :], gout_ref[...])


def _ffn(x, g_in, g_out, w_gate, w_up, w_down, layer, idx, *, tm=1024, tf=FFN_CHUNK):
    t, d = x.shape
    n_chunks = w_gate.shape[-1] // tf
    assert n_chunks >= 3, "first / middle / last d_ff chunks are separate code paths"
    return pl.pallas_call(
        _ffn_body,
        grid=(t // tm, n_chunks),
        in_specs=[
            pl.BlockSpec((tm, d), lambda i, j: (i, 0)),
            pl.BlockSpec((1, d), lambda i, j: (0, 0)),
            pl.BlockSpec((1, d), lambda i, j: (0, 0)),
            pl.BlockSpec((None, None, d, tf), lambda i, j: (layer, idx, 0, j)),
            pl.BlockSpec((None, None, d, tf), lambda i, j: (layer, idx, 0, j)),
            pl.BlockSpec((None, None, tf, d), lambda i, j: (layer, idx, j, 0)),
        ],
        out_specs=pl.BlockSpec((tm, d), lambda i, j: (i, 0)),
        out_shape=jax.ShapeDtypeStruct((t, d), F32),
        scratch_shapes=[pltpu.VMEM((tm, d), BF16)],
        compiler_params=_params("parallel", "arbitrary"),
        name="ffn",
    )(x, g_in, g_out, w_gate, w_up, w_down)


_NT = (((1,), (1,)), ((), ()))


def _even_in_body(x_ref, g_ref, win_ref, gq_ref, gkv_ref, wuqt_ref, wuk_ref, wuvt_ref, gv_ref, tab_ref,
                  tabt_ref, qt_ref, k_ref, vt_ref, u_ref, vl_ref):
    h = _rms(x_ref[...], g_ref[...]).astype(BF16)
    p = jnp.dot(h, win_ref[...], preferred_element_type=F32)
    o_kv = Q_LORA
    o_u = Q_LORA + KV_LORA
    o_v = o_u + GM_WIDTH
    o_r = o_v + GM_WIDTH

    c_q = _rms(p[:, :o_kv], gq_ref[...]).astype(BF16)
    c_kv = _rms(p[:, o_kv:o_u], gkv_ref[...]).astype(BF16)
    qt = lax.dot_general(wuqt_ref[...], c_q, _NT, preferred_element_type=F32)
    vt = lax.dot_general(wuvt_ref[...], c_kv, _NT, preferred_element_type=F32)
    k_nope = jnp.dot(c_kv, wuk_ref[...], preferred_element_type=F32)

    a = p[:, o_r:o_r + 2 * QK_ROPE] * tab_ref[...]
    k_rope = (a + pltpu.roll(a, QK_ROPE, axis=1)).astype(BF16)
    tabt = tabt_ref[...]
    q_scale = (QK_NOPE + QK_ROPE) ** -0.5 * LOG2E
    tm = x_ref.shape[0]
    row = lax.broadcasted_iota(jnp.int32, (MLA_V_ROWS - V_HEAD, MLA_KV_CHUNK), 0)
    ones_rows = jnp.where(row == 0, 1.0, 0.0).astype(BF16)
    for hd in range(MLA_HEADS):
        qo = hd * MLA_QK
        qt_ref[hd, :QK_NOPE, :] = (qt[qo:qo + QK_NOPE] * q_scale).astype(BF16)
        ar = qt[qo + QK_NOPE:qo + MLA_QK] * tabt
        qt_ref[hd, QK_NOPE:QK_NOPE + QK_ROPE, :] = ((ar[:QK_ROPE] + ar[QK_ROPE:]) * q_scale).astype(BF16)
        qt_ref[hd, QK_NOPE + QK_ROPE:, :] = jnp.zeros((QK_ROPE, tm), BF16)
        k_ref[hd, :, :QK_NOPE] = k_nope[:, hd * QK_NOPE:(hd + 1) * QK_NOPE].astype(BF16)
        k_ref[hd, :, QK_NOPE:] = k_rope
        for c in range(tm // MLA_KV_CHUNK):
            vt_ref[hd, c, :V_HEAD, :] = vt[hd * V_HEAD:(hd + 1) * V_HEAD,
                                           c * MLA_KV_CHUNK:(c + 1) * MLA_KV_CHUNK].astype(BF16)
            vt_ref[hd, c, V_HEAD:, :] = ones_rows

    u_ref[...] = jax.nn.gelu(p[:, o_u:o_v]).astype(BF16)
    gv = jax.nn.gelu(p[:, o_v:o_r])
    gc = gv - jnp.mean(gv, axis=-1, keepdims=True)
    ln = gc * lax.rsqrt(jnp.mean(gc * gc, axis=-1, keepdims=True) + EPS) * gv_ref[...]
    vl_ref[...] = ln.astype(BF16)


def _even_in(x, g, w_in, g_q, g_kv, w_uq_t, w_uk, w_uv_t, g_v, tab, seq, *, tm=512):
    t, d = x.shape
    tiles_per_seq = seq // tm
    q_split = MLA_TQ // tm
    tok = lambda i: (i, 0)
    consts = [g, w_in, g_q, g_kv, w_uq_t, w_uk, w_uv_t, g_v]
    return pl.pallas_call(
        _even_in_body,
        grid=(t // tm,),
        in_specs=[pl.BlockSpec((tm, d), tok)] + [_resident(c.shape) for c in consts] + [
            pl.BlockSpec((tm, 2 * QK_ROPE), lambda i: (i % tiles_per_seq, 0)),
            pl.BlockSpec((2 * QK_ROPE, tm), lambda i: (0, i % tiles_per_seq)),
        ],
        out_specs=[
            pl.BlockSpec((MLA_HEADS, None, MLA_QK, tm), lambda i: (0, i // q_split, 0, i % q_split)),
            pl.BlockSpec((MLA_HEADS, tm, MLA_QK), lambda i: (0, i, 0)),
            pl.BlockSpec((MLA_HEADS, tm // MLA_KV_CHUNK, MLA_V_ROWS, MLA_KV_CHUNK), lambda i: (0, i, 0, 0)),
            pl.BlockSpec((tm, GM_WIDTH), tok),
            pl.BlockSpec((tm, GM_WIDTH), tok),
        ],
        out_shape=[
            jax.ShapeDtypeStruct((MLA_HEADS, t // MLA_TQ, MLA_QK, MLA_TQ), BF16),
            jax.ShapeDtypeStruct((MLA_HEADS, t, MLA_QK), BF16),
            jax.ShapeDtypeStruct((MLA_HEADS, t // MLA_KV_CHUNK, MLA_V_ROWS, MLA_KV_CHUNK), BF16),
            jax.ShapeDtypeStruct((t, GM_WIDTH), BF16),
            jax.ShapeDtypeStruct((t, GM_WIDTH), BF16),
        ],
        compiler_params=_params("parallel"),
        name="even_in",
    )(x, *consts, tab, tab.T)


def _mla_body(qt_ref, k_ref, vt_ref, o_ref, acc_ref, s_ref, *, tk):
    n_tiles, _, tq = qt_ref.shape
    sub = tk // MLA_KV_CHUNK
    n_chunks = k_ref.shape[0] // tk
    chunk_bits = n_chunks.bit_length() - 1
    unroll = min(MLA_UNROLL, max(n_chunks // 2, MLA_SLOTS))
    assert n_chunks == 1 << chunk_bits and n_chunks % unroll == 0 and unroll % MLA_SLOTS == 0
    last = n_tiles * n_chunks - 1
    acc_ref[...] = jnp.zeros(acc_ref.shape, F32)

    def scores(v, slot):
        tile = lax.shift_right_logical(v, jnp.int32(chunk_bits))
        off = pl.multiple_of((v & (n_chunks - 1)) * tk, tk)
        st = jnp.dot(k_ref[pl.ds(off, tk), :], qt_ref[tile], preferred_element_type=F32)
        s_ref[slot] = st
        return jnp.max(st, axis=0, keepdims=True)

    def absorb(tile, c, slot, m_chunk, m_old):
        m_new = jnp.maximum(m_old, m_chunk)
        alpha = jnp.exp2(m_old - m_new)
        pb = jnp.exp2(s_ref[slot] - m_new).astype(BF16)
        vt = jnp.concatenate([vt_ref[c * sub + j] for j in range(sub)], axis=1)
        pv = jnp.dot(vt, pb, preferred_element_type=F32)
        acc_ref[tile] = alpha * acc_ref[tile] + pv
        return m_new

    m_chunk = scores(jnp.int32(0), 0)
    for tile in range(n_tiles):
        def step(it, carry, tile=tile):
            m, m_chunk = carry
            base = unroll * it
            for u in range(unroll):
                nxt = tile * n_chunks + base + u + 1
                if u == unroll - 1:
                    nxt = jnp.minimum(nxt, last)
                m_next = scores(nxt, (u + 1) % MLA_SLOTS)
                m = absorb(tile, base + u, u % MLA_SLOTS, m_chunk, m)
                m_chunk = m_next
            return m, m_chunk

        _, m_chunk = lax.fori_loop(0, n_chunks // unroll, step, (jnp.full((1, tq), -jnp.inf, F32), m_chunk))
        out = acc_ref[tile, :V_HEAD, :] / acc_ref[tile, V_HEAD:V_HEAD + 1, :]
        o_ref[tile * tq:(tile + 1) * tq, :] = out.T.astype(o_ref.dtype)


def _mla(qt, k, vt, n_seq, seq, *, tk=512):
    t = n_seq * seq
    span = MLA_Q_TILES * MLA_TQ
    nq = seq // span
    assert seq % span == 0 and seq % tk == 0
    return pl.pallas_call(
        functools.partial(_mla_body, tk=tk),
        grid=(n_seq, MLA_HEADS, nq),
        in_specs=[
            pl.BlockSpec((None, MLA_Q_TILES, MLA_QK, MLA_TQ), lambda b, h, i: (h, b * nq + i, 0, 0)),
            pl.BlockSpec((None, seq, MLA_QK), lambda b, h, i: (h, b, 0)),
            pl.BlockSpec((None, seq // MLA_KV_CHUNK, MLA_V_ROWS, MLA_KV_CHUNK), lambda b, h, i: (h, b, 0, 0)),
        ],
        out_specs=pl.BlockSpec((span, V_HEAD), lambda b, h, i: (b * nq + i, h)),
        out_shape=jax.ShapeDtypeStruct((t, MLA_HEADS * V_HEAD), BF16),
        scratch_shapes=[pltpu.VMEM((MLA_Q_TILES, MLA_V_ROWS, MLA_TQ), F32),
                        pltpu.VMEM((MLA_SLOTS, tk, MLA_TQ), F32)],
        compiler_params=_params("parallel", "parallel", "arbitrary"),
        name="mla_attention",
    )(qt, k, vt)


def _even_out_body(x_ref, a_ref, u_ref, vl_ref, ws_ref, bs_ref, wo_ref, g_ref, o_ref, b_ref):
    tm = x_ref.shape[0]
    for c in range(tm // GM_CHUNK):
        rows = slice(c * GM_CHUNK, (c + 1) * GM_CHUNK)
        for grp in range(GM_GROUPS):
            cols = slice(grp * 128, (grp + 1) * 128)
            mixed = jnp.dot(ws_ref[grp], vl_ref[rows, cols], preferred_element_type=F32)
            mixed = mixed + bs_ref[:, grp:grp + 1]
            b_ref[rows, cols] = (u_ref[rows, cols].astype(F32) * mixed).astype(BF16)
    n_a = a_ref.shape[1]
    y = jnp.dot(a_ref[...], wo_ref[:n_a, :], preferred_element_type=F32)
    y = y + jnp.dot(b_ref[...], wo_ref[n_a:, :], preferred_element_type=F32)
    o_ref[...] = x_ref[...] + _rms(y, g_ref[...])


def _even_out(x, a, u, vl, w_s, b_s_t, w_o, g, *, tm=512):
    t, d = x.shape
    tok = lambda i: (i, 0)
    return pl.pallas_call(
        _even_out_body,
        grid=(t // tm,),
        in_specs=[
            pl.BlockSpec((tm, d), tok),
            pl.BlockSpec((tm, a.shape[1]), tok),
            pl.BlockSpec((tm, GM_WIDTH), tok),
            pl.BlockSpec((tm, GM_WIDTH), tok),
            _resident(w_s.shape), _resident(b_s_t.shape), _resident(w_o.shape), _resident(g.shape),
        ],
        out_specs=pl.BlockSpec((tm, d), tok),
        out_shape=jax.ShapeDtypeStruct((t, d), F32),
        scratch_shapes=[pltpu.VMEM((tm, GM_WIDTH), BF16)],
        compiler_params=_params("parallel"),
        name="even_out",
    )(x, a, u, vl, w_s, b_s_t, w_o, g)


def _odd_in_body(x_ref, g_ref, wqt_ref, wk_ref, wvt_ref, qt_ref, k_ref, vt_ref):
    h = _rms(x_ref[...], g_ref[...]).astype(BF16)
    qt = lax.dot_general(wqt_ref[...], h, _NT, preferred_element_type=F32)
    qt_ref[...] = (qt * (HEAD_DIM ** -0.5 * LOG2E)).astype(BF16)
    k_ref[...] = jnp.dot(h, wk_ref[...], preferred_element_type=F32).astype(BF16)
    vt_ref[...] = lax.dot_general(wvt_ref[...], h, _NT, preferred_element_type=F32).astype(BF16)


def _odd_in(x, g, w_q_t, w_k, w_v_t, *, tm=512):
    t, d = x.shape
    nq, nk = w_q_t.shape[0], w_k.shape[1]
    consts = [g, w_q_t, w_k, w_v_t]
    return pl.pallas_call(
        _odd_in_body,
        grid=(t // tm,),
        in_specs=[pl.BlockSpec((tm, d), lambda i: (i, 0))] + [_resident(c.shape) for c in consts],
        out_specs=[pl.BlockSpec((nq, tm), lambda i: (0, i)), pl.BlockSpec((tm, nk), lambda i: (i, 0)),
                   pl.BlockSpec((nk, tm), lambda i: (0, i))],
        out_shape=[jax.ShapeDtypeStruct((nq, t), BF16), jax.ShapeDtypeStruct((t, nk), BF16),
                   jax.ShapeDtypeStruct((nk, t), BF16)],
        compiler_params=_params("parallel"),
        name="odd_in",
    )(x, *consts)


def _window_body(sink_ref, qt_ref, kp_ref, km_ref, kn_ref, vtp_ref, vtm_ref, vtn_ref, o_ref, k_scr, vt_scr,
                 *, seq):
    blk = WINDOW
    tq = qt_ref.shape[1]
    k_scr[:blk] = kp_ref[...]
    k_scr[blk:blk + tq] = km_ref[...]
    k_scr[blk + tq:] = kn_ref[...]
    vt_scr[:, :blk] = vtp_ref[...]
    vt_scr[:, blk:blk + tq] = vtm_ref[...]
    vt_scr[:, blk + tq:] = vtn_ref[...]

    j_idx = lax.broadcasted_iota(jnp.int32, (3 * blk, blk), 0)
    a_idx = lax.broadcasted_iota(jnp.int32, (3 * blk, blk), 1)
    dist = jnp.abs(j_idx - blk - a_idx)
    tile_start = pl.program_id(1) * tq

    for r in range(tq // blk):
        key_pos = tile_start + (r - 1) * blk + j_idx
        valid = (dist <= WINDOW) & (key_pos >= 0) & (key_pos < seq)
        neg_dist = jnp.where(valid, dist.astype(F32) * -LOG2E, -jnp.inf)
        q_lanes = slice(r * blk, (r + 1) * blk)
        win = slice(r * blk, (r + 3) * blk)
        for kvh in range(GQA_KV_HEADS):
            feat = slice(kvh * HEAD_DIM, (kvh + 1) * HEAD_DIM)
            heads = [kvh * GQA_GROUP + grp for grp in range(GQA_GROUP)]
            qt = jnp.concatenate([qt_ref[hd * HEAD_DIM:(hd + 1) * HEAD_DIM, q_lanes] for hd in heads], axis=1)
            st = jnp.dot(k_scr[win, feat], qt, preferred_element_type=F32)
            bias = jnp.concatenate([2.0 ** (-8.0 * (hd + 1) / GQA_HEADS) * neg_dist for hd in heads], axis=1)
            sink = jnp.concatenate([jnp.full((1, blk), sink_ref[hd] * LOG2E, F32) for hd in heads], axis=1)
            logits = st + bias
            m = jnp.maximum(jnp.max(logits, axis=0, keepdims=True), sink)
            e = jnp.exp2(logits - m)
            denom = jnp.sum(e, axis=0, keepdims=True) + jnp.exp2(sink - m)
            ot = jnp.dot(vt_scr[feat, win], e.astype(BF16), preferred_element_type=F32) / denom
            for grp, hd in enumerate(heads):
                o_ref[q_lanes, hd * HEAD_DIM:(hd + 1) * HEAD_DIM] = (
                    ot[:, grp * blk:(grp + 1) * blk].T.astype(o_ref.dtype))


def _window(qt, k, vt, sink, n_seq, seq, *, tq=512):
    t = n_seq * seq
    nq = seq // tq
    r = tq // WINDOW
    nk = k.shape[1]
    blocks_per_seq = seq // WINDOW
    main = lambda b, i: b * nq + i
    prev = lambda b, i: b * blocks_per_seq + jnp.maximum(i * r - 1, 0)
    nxt = lambda b, i: b * blocks_per_seq + jnp.minimum((i + 1) * r, blocks_per_seq - 1)
    rows = lambda size, idx: pl.BlockSpec((size, nk), lambda b, i: (idx(b, i), 0))
    lanes = lambda size, idx: pl.BlockSpec((nk, size), lambda b, i: (0, idx(b, i)))
    return pl.pallas_call(
        functools.partial(_window_body, seq=seq),
        grid=(n_seq, nq),
        in_specs=[pl.BlockSpec(memory_space=pltpu.SMEM),
                  pl.BlockSpec((qt.shape[0], tq), lambda b, i: (0, main(b, i))),
                  rows(WINDOW, prev), rows(tq, main), rows(WINDOW, nxt),
                  lanes(WINDOW, prev), lanes(tq, main), lanes(WINDOW, nxt)],
        out_specs=pl.BlockSpec((tq, qt.shape[0]), lambda b, i: (main(b, i), 0)),
        out_shape=jax.ShapeDtypeStruct((t, qt.shape[0]), BF16),
        scratch_shapes=[pltpu.VMEM((tq + 2 * WINDOW, nk), BF16), pltpu.VMEM((nk, tq + 2 * WINDOW), BF16)],
        compiler_params=_params("parallel", "parallel"),
        name="window_attention",
    )(sink, qt, k, k, k, vt, vt, vt)


def _odd_out_body(x_ref, a_ref, wo_ref, g_ref, o_ref):
    y = jnp.dot(a_ref[...], wo_ref[...], preferred_element_type=F32)
    o_ref[...] = x_ref[...] + _rms(y, g_ref[...])


def _odd_out(x, a, w_o, g, *, tm=512):
    t, d = x.shape
    tok = lambda i: (i, 0)
    return pl.pallas_call(
        _odd_out_body,
        grid=(t // tm,),
        in_specs=[pl.BlockSpec((tm, d), tok), pl.BlockSpec((tm, a.shape[1]), tok),
                  _resident(w_o.shape), _resident(g.shape)],
        out_specs=pl.BlockSpec((tm, d), tok),
        out_shape=jax.ShapeDtypeStruct((t, d), F32),
        compiler_params=_params("parallel"),
        name="odd_out",
    )(x, a, w_o, g)


def _swap_halves(w):
    half = w.shape[-1] // 2
    return jnp.concatenate([-w[..., half:], w[..., :half]], axis=-1)


def _prep_even(w_in, w_uq, w_ukv):
    o_u = Q_LORA + KV_LORA
    o_g = o_u + QK_ROPE
    k_r = w_in[:, o_u:o_g]
    w_in_x = jnp.concatenate([w_in[:, :o_u], w_in[:, o_g:], k_r, _swap_halves(k_r)], axis=-1)
    per_head = w_uq.reshape(w_uq.shape[0], MLA_HEADS, QK_NOPE + QK_ROPE)
    rope = per_head[..., QK_NOPE:]
    w_uq_x = jnp.concatenate([per_head, _swap_halves(rope)], axis=-1).reshape(w_uq.shape[0], MLA_HEADS * MLA_QK)
    kv = w_ukv.reshape(w_ukv.shape[0], MLA_HEADS, QK_NOPE + V_HEAD)
    w_uk = kv[..., :QK_NOPE].reshape(w_ukv.shape[0], MLA_HEADS * QK_NOPE)
    w_uv = kv[..., QK_NOPE:].reshape(w_ukv.shape[0], MLA_HEADS * V_HEAD)
    return w_in_x.astype(BF16), w_uq_x.T.astype(BF16), w_uk.astype(BF16), w_uv.T.astype(BF16)


def _prep_odd(w_qkv):
    nq = GQA_HEADS * HEAD_DIM
    nk = GQA_KV_HEADS * HEAD_DIM
    return (w_qkv[:, :nq].T.astype(BF16), w_qkv[:, nq:nq + nk].astype(BF16), w_qkv[:, nq + nk:].T.astype(BF16))


def _rope_table(seq):
    pos = jnp.arange(seq, dtype=F32)
    inv = ROPE_THETA ** (-jnp.arange(0, QK_ROPE, 2, dtype=F32) / QK_ROPE)
    ang = pos[:, None] * inv[None, :]
    cos, sin = jnp.cos(ang), jnp.sin(ang)
    return jnp.concatenate([cos, cos, sin, sin], axis=-1)


def _trunk(x3, w):
    n_seq, seq, d = x3.shape
    x = x3.reshape(n_seq * seq, d)
    depth = w["norm_g"].shape[0]
    for layer in range(depth):
        g = w["norm_g"][layer]
        row = lambda i: g[i][None, :]
        ffn = lambda x, n, idx: _ffn(x, row(n), row(n + 1), w["gate"], w["up"], w["down"], layer, idx)
        x = ffn(x, 0, 0)
        i = layer // 2
        if layer % 2 == 0:
            w_in, w_uq_t, w_uk, w_uv_t = w["even"][i]
            qt, k, vt, u, vl = _even_in(x, row(2), w_in, w["g_q"][i][None, :], w["g_kv"][i][None, :],
                                        w_uq_t, w_uk, w_uv_t, w["g_v"][i][None, :], _rope_table(seq), seq)
            a = _mla(qt, k, vt, n_seq, seq)
            x = _even_out(x, a, u, vl, w["w_s"][i], w["b_s_t"][i], w["even_o"][i], row(3))
        else:
            qt, k, vt = _odd_in(x, row(2), *w["odd"][i])
            a = _window(qt, k, vt, w["sink"][i], n_seq, seq)
            x = _odd_out(x, a, w["odd_o"][i], row(3))
        x = ffn(x, 4, 1)
    return x.reshape(n_seq, seq, d)


def kernel(x_prompt, x_sample, norm_g, ffn_w_gate, ffn_w_up, ffn_w_down, even_w_in, mla_g_q, mla_g_kv, mla_w_uq, mla_w_ukv, gm_g_v, gm_w_s, gm_b_s, even_w_o, odd_w_qkv, odd_sink, odd_w_o):
    even = [_prep_even(even_w_in[i], mla_w_uq[i], mla_w_ukv[i]) for i in range(even_w_in.shape[0])]
    w = {
        "norm_g": norm_g,
        "gate": ffn_w_gate.astype(BF16), "up": ffn_w_up.astype(BF16), "down": ffn_w_down.astype(BF16),
        "even": even, "g_q": mla_g_q, "g_kv": mla_g_kv, "g_v": gm_g_v,
        "w_s": gm_w_s.astype(BF16), "b_s_t": jnp.swapaxes(gm_b_s, 1, 2), "even_o": even_w_o.astype(BF16),
        "odd": [_prep_odd(odd_w_qkv[i]) for i in range(odd_w_qkv.shape[0])],
        "sink": odd_sink, "odd_o": odd_w_o.astype(BF16),
    }
    return (_trunk(x_prompt, w), _trunk(x_sample, w))
```

```python
import functools

import jax
import jax.numpy as jnp
from jax import lax
from jax.experimental import pallas as pl
from jax.experimental.pallas import tpu as pltpu

F32 = jnp.float32
BF16 = jnp.bfloat16

EPS = 1e-6
LOG2E = 1.4426950408889634

MLA_HEADS = 8
Q_LORA = 768
KV_LORA = 512
QK_NOPE = 128
QK_ROPE = 64
V_HEAD = 128
ROPE_THETA = 10000.0
MLA_QK = QK_NOPE + 2 * QK_ROPE
MLA_KV_CHUNK = 256
MLA_TQ = 1024
MLA_Q_TILES = 8
MLA_SLOTS = 4
MLA_V_ROWS = V_HEAD + 16

GM_GROUPS = 8
GM_CHUNK = 128
GM_WIDTH = GM_GROUPS * 128

GQA_HEADS = 16
GQA_KV_HEADS = 4
GQA_GROUP = GQA_HEADS // GQA_KV_HEADS
HEAD_DIM = 128
WINDOW = 128

FFN_ROW_SPLIT = 2
FFN_CHUNK = 512

VMEM_LIMIT_BYTES = 56 * 1024 * 1024


def _params(*semantics):
    return pltpu.CompilerParams(dimension_semantics=semantics, vmem_limit_bytes=VMEM_LIMIT_BYTES)


def _resident(shape):
    zeros = (0,) * len(shape)
    return pl.BlockSpec(shape, lambda *_: zeros, pipeline_mode=pl.Buffered(1))


def _rms(x, g):
    return x * lax.rsqrt(jnp.mean(x * x, axis=-1, keepdims=True) + EPS) * g


def _ffn_body(x_ref, gin_ref, gout_ref, wg_ref, wu_ref, wd_ref, o_ref, xn_ref):
    acc_ref = o_ref
    j = pl.program_id(1)
    last = pl.num_programs(1) - 1
    tr = x_ref.shape[0] // FFN_ROW_SPLIT
    row_blocks = [slice(r * tr, (r + 1) * tr) for r in range(FFN_ROW_SPLIT)]

    def chunk(rows, first):
        xn = xn_ref[rows, :]
        gate = jnp.dot(xn, wg_ref[...], preferred_element_type=F32)
        up = jnp.dot(xn, wu_ref[...], preferred_element_type=F32)
        h = (gate * jax.nn.sigmoid(gate) * up).astype(BF16)
        part = jnp.dot(h, wd_ref[...], preferred_element_type=F32)
        if first:
            acc_ref[rows, :] = part
        else:
            acc_ref[rows, :] += part

    @pl.when(j == 0)
    def _():
        for rows in row_blocks:
            xn_ref[rows, :] = _rms(x_ref[rows, :], gin_ref[...]).astype(BF16)
            chunk(rows, True)

    @pl.when((j > 0) & (j < last))
    def _():
        for rows in row_blocks:
            chunk(rows, False)

    @pl.when(j == last)
    def _():
        for rows in row_blocks:
            chunk(rows, False)
            o_ref[rows, :] = x_ref[rows, :] + 0.5 * _rms(acc_ref[rows, :], gout_ref[...])


def _ffn(x, g_in, g_out, w_gate, w_up, w_down, layer, idx, *, tm=1024, tf=FFN_CHUNK):
    t, d = x.shape
    n_chunks = w_gate.shape[-1] // tf
    assert n_chunks >= 3, "first / middle / last d_ff chunks are separate code paths"
    return pl.pallas_call(
        _ffn_body,
        grid=(t // tm, n_chunks),
        in_specs=[
            pl.BlockSpec((tm, d), lambda i, j: (i, 0)),
            pl.BlockSpec((1, d), lambda i, j: (0, 0)),
            pl.BlockSpec((1, d), lambda i, j: (0, 0)),
            pl.BlockSpec((None, None, d, tf), lambda i, j: (layer, idx, 0, j)),
            pl.BlockSpec((None, None, d, tf), lambda i, j: (layer, idx, 0, j)),
            pl.BlockSpec((None, None, tf, d), lambda i, j: (layer, idx, j, 0)),
        ],
        out_specs=pl.BlockSpec((tm, d), lambda i, j: (i, 0)),
        out_shape=jax.ShapeDtypeStruct((t, d), F32),
        scratch_shapes=[pltpu.VMEM((tm, d), BF16)],
        compiler_params=_params("parallel", "arbitrary"),
        name="ffn",
    )(x, g_in, g_out, w_gate, w_up, w_down)


_NT = (((1,), (1,)), ((), ()))


def _even_in_body(x_ref, g_ref, win_ref, gq_ref, gkv_ref, wuqt_ref, wuk_ref, wuvt_ref, gv_ref, tab_ref,
                  tabt_ref, qt_ref, k_ref, vt_ref, u_ref, vl_ref):
    h = _rms(x_ref[...], g_ref[...]).astype(BF16)
    p = jnp.dot(h, win_ref[...], preferred_element_type=F32)
    o_kv = Q_LORA
    o_u = Q_LORA + KV_LORA
    o_v = o_u + GM_WIDTH
    o_r = o_v + GM_WIDTH

    c_q = _rms(p[:, :o_kv], gq_ref[...]).astype(BF16)
    c_kv = _rms(p[:, o_kv:o_u], gkv_ref[...]).astype(BF16)
    qt = lax.dot_general(wuqt_ref[...], c_q, _NT, preferred_element_type=F32)
    vt = lax.dot_general(wuvt_ref[...], c_kv, _NT, preferred_element_type=F32)
    k_nope = jnp.dot(c_kv, wuk_ref[...], preferred_element_type=F32)

    a = p[:, o_r:o_r + 2 * QK_ROPE] * tab_ref[...]
    k_rope = (a + pltpu.roll(a, QK_ROPE, axis=1)).astype(BF16)
    tabt = tabt_ref[...]
    q_scale = (QK_NOPE + QK_ROPE) ** -0.5 * LOG2E
    tm = x_ref.shape[0]
    row = lax.broadcasted_iota(jnp.int32, (MLA_V_ROWS - V_HEAD, MLA_KV_CHUNK), 0)
    ones_rows = jnp.where(row == 0, 1.0, 0.0).astype(BF16)
    for hd in range(MLA_HEADS):
        qo = hd * MLA_QK
        qt_ref[hd, :QK_NOPE, :] = (qt[qo:qo + QK_NOPE] * q_scale).astype(BF16)
        ar = qt[qo + QK_NOPE:qo + MLA_QK] * tabt
        qt_ref[hd, QK_NOPE:QK_NOPE + QK_ROPE, :] = ((ar[:QK_ROPE] + ar[QK_ROPE:]) * q_scale).astype(BF16)
        qt_ref[hd, QK_NOPE + QK_ROPE:, :] = jnp.zeros((QK_ROPE, tm), BF16)
        k_ref[hd, :, :QK_NOPE] = k_nope[:, hd * QK_NOPE:(hd + 1) * QK_NOPE].astype(BF16)
        k_ref[hd, :, QK_NOPE:] = k_rope
        for c in range(tm // MLA_KV_CHUNK):
            vt_ref[hd, c, :V_HEAD, :] = vt[hd * V_HEAD:(hd + 1) * V_HEAD,
                                           c * MLA_KV_CHUNK:(c + 1) * MLA_KV_CHUNK].astype(BF16)
            vt_ref[hd, c, V_HEAD:, :] = ones_rows

    u_ref[...] = jax.nn.gelu(p[:, o_u:o_v]).astype(BF16)
    gv = jax.nn.gelu(p[:, o_v:o_r])
    gc = gv - jnp.mean(gv, axis=-1, keepdims=True)
    ln = gc * lax.rsqrt(jnp.mean(gc * gc, axis=-1, keepdims=True) + EPS) * gv_ref[...]
    vl_ref[...] = ln.astype(BF16)


def _even_in(x, g, w_in, g_q, g_kv, w_uq_t, w_uk, w_uv_t, g_v, tab, seq, *, tm=512):
    t, d = x.shape
    tiles_per_seq = seq // tm
    q_split = MLA_TQ // tm
    tok = lambda i: (i, 0)
    consts = [g, w_in, g_q, g_kv, w_uq_t, w_uk, w_uv_t, g_v]
    return pl.pallas_call(
        _even_in_body,
        grid=(t // tm,),
        in_specs=[pl.BlockSpec((tm, d), tok)] + [_resident(c.shape) for c in consts] + [
            pl.BlockSpec((tm, 2 * QK_ROPE), lambda i: (i % tiles_per_seq, 0)),
            pl.BlockSpec((2 * QK_ROPE, tm), lambda i: (0, i % tiles_per_seq)),
        ],
        out_specs=[
            pl.BlockSpec((MLA_HEADS, None, MLA_QK, tm), lambda i: (0, i // q_split, 0, i % q_split)),
            pl.BlockSpec((MLA_HEADS, tm, MLA_QK), lambda i: (0, i, 0)),
            pl.BlockSpec((MLA_HEADS, tm // MLA_KV_CHUNK, MLA_V_ROWS, MLA_KV_CHUNK), lambda i: (0, i, 0, 0)),
            pl.BlockSpec((tm, GM_WIDTH), tok),
            pl.BlockSpec((tm, GM_WIDTH), tok),
        ],
        out_shape=[
            jax.ShapeDtypeStruct((MLA_HEADS, t // MLA_TQ, MLA_QK, MLA_TQ), BF16),
            jax.ShapeDtypeStruct((MLA_HEADS, t, MLA_QK), BF16),
            jax.ShapeDtypeStruct((MLA_HEADS, t // MLA_KV_CHUNK, MLA_V_ROWS, MLA_KV_CHUNK), BF16),
            jax.ShapeDtypeStruct((t, GM_WIDTH), BF16),
            jax.ShapeDtypeStruct((t, GM_WIDTH), BF16),
        ],
        compiler_params=_params("parallel"),
        name="even_in",
    )(x, *consts, tab, tab.T)


def _mla_body(qt_ref, k_ref, vt_ref, o_ref, acc_ref, s_ref, *, tk):
    n_tiles, _, tq = qt_ref.shape
    sub = tk // MLA_KV_CHUNK
    n_chunks = k_ref.shape[0] // tk
    chunk_bits = n_chunks.bit_length() - 1
    assert n_chunks == 1 << chunk_bits and n_chunks % MLA_SLOTS == 0
    last = n_tiles * n_chunks - 1

    def scores(v, slot):
        tile = lax.shift_right_logical(v, jnp.int32(chunk_bits))
        off = pl.multiple_of((v & (n_chunks - 1)) * tk, tk)
        st = jnp.dot(k_ref[pl.ds(off, tk), :], qt_ref[tile], preferred_element_type=F32)
        s_ref[slot] = st
        return jnp.max(st, axis=0, keepdims=True)

    def absorb(c, slot, m_chunk, m_old):
        m_new = jnp.maximum(m_old, m_chunk)
        alpha = jnp.exp2(m_old - m_new)
        pb = jnp.exp2(s_ref[slot] - m_new).astype(BF16)
        vt = jnp.concatenate([vt_ref[c * sub + j] for j in range(sub)], axis=1)
        pv = jnp.dot(vt, pb, preferred_element_type=F32)
        acc_ref[...] = alpha * acc_ref[...] + pv
        return m_new

    def tile_step(tile, m_chunk):
        acc_ref[...] = jnp.zeros(acc_ref.shape, F32)
        m = jnp.full((1, tq), -jnp.inf, F32)
        for c in range(n_chunks):
            nxt = tile * n_chunks + c + 1
            if c == n_chunks - 1:
                nxt = jnp.minimum(nxt, last)
            m_next = scores(nxt, (c + 1) % MLA_SLOTS)
            m = absorb(c, c % MLA_SLOTS, m_chunk, m)
            m_chunk = m_next
        out = acc_ref[:V_HEAD, :] / acc_ref[V_HEAD:V_HEAD + 1, :]
        o_ref[pl.ds(pl.multiple_of(tile * tq, tq), tq), :] = out.T.astype(o_ref.dtype)
        return m_chunk

    lax.fori_loop(0, n_tiles, tile_step, scores(jnp.int32(0), 0))


def _mla(qt, k, vt, n_seq, seq, *, tk=512):
    t = n_seq * seq
    span = MLA_Q_TILES * MLA_TQ
    nq = seq // span
    assert seq % span == 0 and seq % tk == 0
    return pl.pallas_call(
        functools.partial(_mla_body, tk=tk),
        grid=(n_seq, MLA_HEADS, nq),
        in_specs=[
            pl.BlockSpec((None, MLA_Q_TILES, MLA_QK, MLA_TQ), lambda b, h, i: (h, b * nq + i, 0, 0)),
            pl.BlockSpec((None, seq, MLA_QK), lambda b, h, i: (h, b, 0)),
            pl.BlockSpec((None, seq // MLA_KV_CHUNK, MLA_V_ROWS, MLA_KV_CHUNK), lambda b, h, i: (h, b, 0, 0)),
        ],
        out_specs=pl.BlockSpec((span, V_HEAD), lambda b, h, i: (b * nq + i, h)),
        out_shape=jax.ShapeDtypeStruct((t, MLA_HEADS * V_HEAD), BF16),
        scratch_shapes=[pltpu.VMEM((MLA_V_ROWS, MLA_TQ), F32), pltpu.VMEM((MLA_SLOTS, tk, MLA_TQ), F32)],
        compiler_params=_params("parallel", "parallel", "arbitrary"),
        name="mla_attention",
    )(qt, k, vt)


def _even_out_body(x_ref, a_ref, u_ref, vl_ref, ws_ref, bs_ref, wo_ref, g_ref, o_ref, b_ref):
    tm = x_ref.shape[0]
    for c in range(tm // GM_CHUNK):
        rows = slice(c * GM_CHUNK, (c + 1) * GM_CHUNK)
        for grp in range(GM_GROUPS):
            cols = slice(grp * 128, (grp + 1) * 128)
            mixed = jnp.dot(ws_ref[grp], vl_ref[rows, cols], preferred_element_type=F32)
            mixed = mixed + bs_ref[:, grp:grp + 1]
            b_ref[rows, cols] = (u_ref[rows, cols].astype(F32) * mixed).astype(BF16)
    n_a = a_ref.shape[1]
    y = jnp.dot(a_ref[...], wo_ref[:n_a, :], preferred_element_type=F32)
    y = y + jnp.dot(b_ref[...], wo_ref[n_a:, :], preferred_element_type=F32)
    o_ref[...] = x_ref[...] + _rms(y, g_ref[...])


def _even_out(x, a, u, vl, w_s, b_s_t, w_o, g, *, tm=512):
    t, d = x.shape
    tok = lambda i: (i, 0)
    return pl.pallas_call(
        _even_out_body,
        grid=(t // tm,),
        in_specs=[
            pl.BlockSpec((tm, d), tok),
            pl.BlockSpec((tm, a.shape[1]), tok),
            pl.BlockSpec((tm, GM_WIDTH), tok),
            pl.BlockSpec((tm, GM_WIDTH), tok),
            _resident(w_s.shape), _resident(b_s_t.shape), _resident(w_o.shape), _resident(g.shape),
        ],
        out_specs=pl.BlockSpec((tm, d), tok),
        out_shape=jax.ShapeDtypeStruct((t, d), F32),
        scratch_shapes=[pltpu.VMEM((tm, GM_WIDTH), BF16)],
        compiler_params=_params("parallel"),
        name="even_out",
    )(x, a, u, vl, w_s, b_s_t, w_o, g)


def _odd_in_body(x_ref, g_ref, wqt_ref, wk_ref, wvt_ref, qt_ref, k_ref, vt_ref):
    h = _rms(x_ref[...], g_ref[...]).astype(BF16)
    qt = lax.dot_general(wqt_ref[...], h, _NT, preferred_element_type=F32)
    qt_ref[...] = (qt * (HEAD_DIM ** -0.5 * LOG2E)).astype(BF16)
    k_ref[...] = jnp.dot(h, wk_ref[...], preferred_element_type=F32).astype(BF16)
    vt_ref[...] = lax.dot_general(wvt_ref[...], h, _NT, preferred_element_type=F32).astype(BF16)


def _odd_in(x, g, w_q_t, w_k, w_v_t, *, tm=512):
    t, d = x.shape
    nq, nk = w_q_t.shape[0], w_k.shape[1]
    consts = [g, w_q_t, w_k, w_v_t]
    return pl.pallas_call(
        _odd_in_body,
        grid=(t // tm,),
        in_specs=[pl.BlockSpec((tm, d), lambda i: (i, 0))] + [_resident(c.shape) for c in consts],
        out_specs=[pl.BlockSpec((nq, tm), lambda i: (0, i)), pl.BlockSpec((tm, nk), lambda i: (i, 0)),
                   pl.BlockSpec((nk, tm), lambda i: (0, i))],
        out_shape=[jax.ShapeDtypeStruct((nq, t), BF16), jax.ShapeDtypeStruct((t, nk), BF16),
                   jax.ShapeDtypeStruct((nk, t), BF16)],
        compiler_params=_params("parallel"),
        name="odd_in",
    )(x, *consts)


def _window_body(sink_ref, qt_ref, kp_ref, km_ref, kn_ref, vtp_ref, vtm_ref, vtn_ref, o_ref, k_scr, vt_scr,
                 *, seq):
    blk = WINDOW
    tq = qt_ref.shape[1]
    k_scr[:blk] = kp_ref[...]
    k_scr[blk:blk + tq] = km_ref[...]
    k_scr[blk + tq:] = kn_ref[...]
    vt_scr[:, :blk] = vtp_ref[...]
    vt_scr[:, blk:blk + tq] = vtm_ref[...]
    vt_scr[:, blk + tq:] = vtn_ref[...]

    j_idx = lax.broadcasted_iota(jnp.int32, (3 * blk, blk), 0)
    a_idx = lax.broadcasted_iota(jnp.int32, (3 * blk, blk), 1)
    dist = jnp.abs(j_idx - blk - a_idx)
    tile_start = pl.program_id(1) * tq

    for r in range(tq // blk):
        key_pos = tile_start + (r - 1) * blk + j_idx
        valid = (dist <= WINDOW) & (key_pos >= 0) & (key_pos < seq)
        neg_dist = jnp.where(valid, dist.astype(F32) * -LOG2E, -jnp.inf)
        q_lanes = slice(r * blk, (r + 1) * blk)
        win = slice(r * blk, (r + 3) * blk)
        for kvh in range(GQA_KV_HEADS):
            feat = slice(kvh * HEAD_DIM, (kvh + 1) * HEAD_DIM)
            heads = [kvh * GQA_GROUP + grp for grp in range(GQA_GROUP)]
            qt = jnp.concatenate([qt_ref[hd * HEAD_DIM:(hd + 1) * HEAD_DIM, q_lanes] for hd in heads], axis=1)
            st = jnp.dot(k_scr[win, feat], qt, preferred_element_type=F32)
            bias = jnp.concatenate([2.0 ** (-8.0 * (hd + 1) / GQA_HEADS) * neg_dist for hd in heads], axis=1)
            sink = jnp.concatenate([jnp.full((1, blk), sink_ref[hd] * LOG2E, F32) for hd in heads], axis=1)
            logits = st + bias
            m = jnp.maximum(jnp.max(logits, axis=0, keepdims=True), sink)
            e = jnp.exp2(logits - m)
            denom = jnp.sum(e, axis=0, keepdims=True) + jnp.exp2(sink - m)
            ot = jnp.dot(vt_scr[feat, win], e.astype(BF16), preferred_element_type=F32) / denom
            for grp, hd in enumerate(heads):
                o_ref[q_lanes, hd * HEAD_DIM:(hd + 1) * HEAD_DIM] = (
                    ot[:, grp * blk:(grp + 1) * blk].T.astype(o_ref.dtype))


def _window(qt, k, vt, sink, n_seq, seq, *, tq=512):
    t = n_seq * seq
    nq = seq // tq
    r = tq // WINDOW
    nk = k.shape[1]
    blocks_per_seq = seq // WINDOW
    main = lambda b, i: b * nq + i
    prev = lambda b, i: b * blocks_per_seq + jnp.maximum(i * r - 1, 0)
    nxt = lambda b, i: b * blocks_per_seq + jnp.minimum((i + 1) * r, blocks_per_seq - 1)
    rows = lambda size, idx: pl.BlockSpec((size, nk), lambda b, i: (idx(b, i), 0))
    lanes = lambda size, idx: pl.BlockSpec((nk, size), lambda b, i: (0, idx(b, i)))
    return pl.pallas_call(
        functools.partial(_window_body, seq=seq),
        grid=(n_seq, nq),
        in_specs=[pl.BlockSpec(memory_space=pltpu.SMEM),
                  pl.BlockSpec((qt.shape[0], tq), lambda b, i: (0, main(b, i))),
                  rows(WINDOW, prev), rows(tq, main), rows(WINDOW, nxt),
                  lanes(WINDOW, prev), lanes(tq, main), lanes(WINDOW, nxt)],
        out_specs=pl.BlockSpec((tq, qt.shape[0]), lambda b, i: (main(b, i), 0)),
        out_shape=jax.ShapeDtypeStruct((t, qt.shape[0]), BF16),
        scratch_shapes=[pltpu.VMEM((tq + 2 * WINDOW, nk), BF16), pltpu.VMEM((nk, tq + 2 * WINDOW), BF16)],
        compiler_params=_params("parallel", "parallel"),
        name="window_attention",
    )(sink, qt, k, k, k, vt, vt, vt)


def _odd_out_body(x_ref, a_ref, wo_ref, g_ref, o_ref):
    y = jnp.dot(a_ref[...], wo_ref[...], preferred_element_type=F32)
    o_ref[...] = x_ref[...] + _rms(y, g_ref[...])


def _odd_out(x, a, w_o, g, *, tm=512):
    t, d = x.shape
    tok = lambda i: (i, 0)
    return pl.pallas_call(
        _odd_out_body,
        grid=(t // tm,),
        in_specs=[pl.BlockSpec((tm, d), tok), pl.BlockSpec((tm, a.shape[1]), tok),
                  _resident(w_o.shape), _resident(g.shape)],
        out_specs=pl.BlockSpec((tm, d), tok),
        out_shape=jax.ShapeDtypeStruct((t, d), F32),
        compiler_params=_params("parallel"),
        name="odd_out",
    )(x, a, w_o, g)


def _swap_halves(w):
    half = w.shape[-1] // 2
    return jnp.concatenate([-w[..., half:], w[..., :half]], axis=-1)


def _prep_even(w_in, w_uq, w_ukv):
    o_u = Q_LORA + KV_LORA
    o_g = o_u + QK_ROPE
    k_r = w_in[:, o_u:o_g]
    w_in_x = jnp.concatenate([w_in[:, :o_u], w_in[:, o_g:], k_r, _swap_halves(k_r)], axis=-1)
    per_head = w_uq.reshape(w_uq.shape[0], MLA_HEADS, QK_NOPE + QK_ROPE)
    rope = per_head[..., QK_NOPE:]
    w_uq_x = jnp.concatenate([per_head, _swap_halves(rope)], axis=-1).reshape(w_uq.shape[0], MLA_HEADS * MLA_QK)
    kv = w_ukv.reshape(w_ukv.shape[0], MLA_HEADS, QK_NOPE + V_HEAD)
    w_uk = kv[..., :QK_NOPE].reshape(w_ukv.shape[0], MLA_HEADS * QK_NOPE)
    w_uv = kv[..., QK_NOPE:].reshape(w_ukv.shape[0], MLA_HEADS * V_HEAD)
    return w_in_x.astype(BF16), w_uq_x.T.astype(BF16), w_uk.astype(BF16), w_uv.T.astype(BF16)


def _prep_odd(w_qkv):
    nq = GQA_HEADS * HEAD_DIM
    nk = GQA_KV_HEADS * HEAD_DIM
    return (w_qkv[:, :nq].T.astype(BF16), w_qkv[:, nq:nq + nk].astype(BF16), w_qkv[:, nq + nk:].T.astype(BF16))


def _rope_table(seq):
    pos = jnp.arange(seq, dtype=F32)
    inv = ROPE_THETA ** (-jnp.arange(0, QK_ROPE, 2, dtype=F32) / QK_ROPE)
    ang = pos[:, None] * inv[None, :]
    cos, sin = jnp.cos(ang), jnp.sin(ang)
    return jnp.concatenate([cos, cos, sin, sin], axis=-1)


def _trunk(x3, w):
    n_seq, seq, d = x3.shape
    x = x3.reshape(n_seq * seq, d)
    depth = w["norm_g"].shape[0]
    for layer in range(depth):
        g = w["norm_g"][layer]
        row = lambda i: g[i][None, :]
        ffn = lambda x, n, idx: _ffn(x, row(n), row(n + 1), w["gate"], w["up"], w["down"], layer, idx)
        x = ffn(x, 0, 0)
        i = layer // 2
        if layer % 2 == 0:
            w_in, w_uq_t, w_uk, w_uv_t = w["even"][i]
            qt, k, vt, u, vl = _even_in(x, row(2), w_in, w["g_q"][i][None, :], w["g_kv"][i][None, :],
                                        w_uq_t, w_uk, w_uv_t, w["g_v"][i][None, :], _rope_table(seq), seq)
            a = _mla(qt, k, vt, n_seq, seq)
            x = _even_out(x, a, u, vl, w["w_s"][i], w["b_s_t"][i], w["even_o"][i], row(3))
        else:
            qt, k, vt = _odd_in(x, row(2), *w["odd"][i])
            a = _window(qt, k, vt, w["sink"][i], n_seq, seq)
            x = _odd_out(x, a, w["odd_o"][i], row(3))
        x = ffn(x, 4, 1)
    return x.reshape(n_seq, seq, d)


def kernel(x_prompt, x_sample, norm_g, ffn_w_gate, ffn_w_up, ffn_w_down, even_w_in, mla_g_q, mla_g_kv, mla_w_uq, mla_w_ukv, gm_g_v, gm_w_s, gm_b_s, even_w_o, odd_w_qkv, odd_sink, odd_w_o):
    even = [_prep_even(even_w_in[i], mla_w_uq[i], mla_w_ukv[i]) for i in range(even_w_in.shape[0])]
    w = {
        "norm_g": norm_g,
        "gate": ffn_w_gate.astype(BF16), "up": ffn_w_up.astype(BF16), "down": ffn_w_down.astype(BF16),
        "even": even, "g_q": mla_g_q, "g_kv": mla_g_kv, "g_v": gm_g_v,
        "w_s": gm_w_s.astype(BF16), "b_s_t": jnp.swapaxes(gm_b_s, 1, 2), "even_o": even_w_o.astype(BF16),
        "odd": [_prep_odd(odd_w_qkv[i]) for i in range(odd_w_qkv.shape[0])],
        "sink": odd_sink, "odd_o": odd_w_o.astype(BF16),
    }
    return (_trunk(x_prompt, w), _trunk(x_sample, w))
```

```python
import functools

import jax
import jax.numpy as jnp
from jax import lax
from jax.experimental import pallas as pl
from jax.experimental.pallas import tpu as pltpu

F32 = jnp.float32
BF16 = jnp.bfloat16

EPS = 1e-6
LOG2E = 1.4426950408889634

MLA_HEADS = 8
Q_LORA = 768
KV_LORA = 512
QK_NOPE = 128
QK_ROPE = 64
V_HEAD = 128
ROPE_THETA = 10000.0
MLA_QK = QK_NOPE + 2 * QK_ROPE
MLA_KV_CHUNK = 256
MLA_TQ = 1024
MLA_Q_TILES = 8
MLA_SLOTS = 4
MLA_V_ROWS = V_HEAD + 16

GM_GROUPS = 8
GM_CHUNK = 128
GM_WIDTH = GM_GROUPS * 128

GQA_HEADS = 16
GQA_KV_HEADS = 4
GQA_GROUP = GQA_HEADS // GQA_KV_HEADS
HEAD_DIM = 128
WINDOW = 128

FFN_ROW_SPLIT = 2
FFN_CHUNK = 512

VMEM_LIMIT_BYTES = 56 * 1024 * 1024


def _params(*semantics):
    return pltpu.CompilerParams(dimension_semantics=semantics, vmem_limit_bytes=VMEM_LIMIT_BYTES)


def _resident(shape):
    zeros = (0,) * len(shape)
    return pl.BlockSpec(shape, lambda *_: zeros, pipeline_mode=pl.Buffered(1))


def _rms(x, g):
    return x * lax.rsqrt(jnp.mean(x * x, axis=-1, keepdims=True) + EPS) * g


def _ffn_body(x_ref, gin_ref, gout_ref, wg_ref, wu_ref, wd_ref, o_ref, xn_ref):
    acc_ref = o_ref
    j = pl.program_id(1)
    last = pl.num_programs(1) - 1
    tr = x_ref.shape[0] // FFN_ROW_SPLIT
    row_blocks = [slice(r * tr, (r + 1) * tr) for r in range(FFN_ROW_SPLIT)]

    def chunk(rows, first):
        xn = xn_ref[rows, :]
        gate = jnp.dot(xn, wg_ref[...], preferred_element_type=F32)
        up = jnp.dot(xn, wu_ref[...], preferred_element_type=F32)
        h = (gate * jax.nn.sigmoid(gate) * up).astype(BF16)
        part = jnp.dot(h, wd_ref[...], preferred_element_type=F32)
        if first:
            acc_ref[rows, :] = part
        else:
            acc_ref[rows, :] += part

    @pl.when(j == 0)
    def _():
        for rows in row_blocks:
            xn_ref[rows, :] = _rms(x_ref[rows, :], gin_ref[...]).astype(BF16)
            chunk(rows, True)

    @pl.when((j > 0) & (j < last))
    def _():
        for rows in row_blocks:
            chunk(rows, False)

    @pl.when(j == last)
    def _():
        for rows in row_blocks:
            chunk(rows, False)
            o_ref[rows, :] = x_ref[rows, :] + 0.5 * _rms(acc_ref[rows, :], gout_ref[...])


def _ffn(x, g_in, g_out, w_gate, w_up, w_down, layer, idx, *, tm=1024, tf=FFN_CHUNK):
    t, d = x.shape
    n_chunks = w_gate.shape[-1] // tf
    assert n_chunks >= 3, "first / middle / last d_ff chunks are separate code paths"
    return pl.pallas_call(
        _ffn_body,
        grid=(t // tm, n_chunks),
        in_specs=[
            pl.BlockSpec((tm, d), lambda i, j: (i, 0)),
            pl.BlockSpec((1, d), lambda i, j: (0, 0)),
            pl.BlockSpec((1, d), lambda i, j: (0, 0)),
            pl.BlockSpec((None, None, d, tf), lambda i, j: (layer, idx, 0, j)),
            pl.BlockSpec((None, None, d, tf), lambda i, j: (layer, idx, 0, j)),
            pl.BlockSpec((None, None, tf, d), lambda i, j: (layer, idx, j, 0)),
        ],
        out_specs=pl.BlockSpec((tm, d), lambda i, j: (i, 0)),
        out_shape=jax.ShapeDtypeStruct((t, d), F32),
        scratch_shapes=[pltpu.VMEM((tm, d), BF16)],
        compiler_params=_params("parallel", "arbitrary"),
        name="ffn",
    )(x, g_in, g_out, w_gate, w_up, w_down)


_NT = (((1,), (1,)), ((), ()))


def _even_in_body(x_ref, g_ref, win_ref, gq_ref, gkv_ref, wuqt_ref, wuk_ref, wuvt_ref, gv_ref, tab_ref,
                  tabt_ref, qt_ref, k_ref, vt_ref, u_ref, vl_ref):
    h = _rms(x_ref[...], g_ref[...]).astype(BF16)
    p = jnp.dot(h, win_ref[...], preferred_element_type=F32)
    o_kv = Q_LORA
    o_u = Q_LORA + KV_LORA
    o_v = o_u + GM_WIDTH
    o_r = o_v + GM_WIDTH

    c_q = _rms(p[:, :o_kv], gq_ref[...]).astype(BF16)
    c_kv = _rms(p[:, o_kv:o_u], gkv_ref[...]).astype(BF16)
    qt = lax.dot_general(wuqt_ref[...], c_q, _NT, preferred_element_type=F32)
    vt = lax.dot_general(wuvt_ref[...], c_kv, _NT, preferred_element_type=F32)
    k_nope = jnp.dot(c_kv, wuk_ref[...], preferred_element_type=F32)

    a = p[:, o_r:o_r + 2 * QK_ROPE] * tab_ref[...]
    k_rope = (a + pltpu.roll(a, QK_ROPE, axis=1)).astype(BF16)
    tabt = tabt_ref[...]
    q_scale = (QK_NOPE + QK_ROPE) ** -0.5 * LOG2E
    tm = x_ref.shape[0]
    row = lax.broadcasted_iota(jnp.int32, (MLA_V_ROWS - V_HEAD, MLA_KV_CHUNK), 0)
    ones_rows = jnp.where(row == 0, 1.0, 0.0).astype(BF16)
    for hd in range(MLA_HEADS):
        qo = hd * MLA_QK
        qt_ref[hd, :QK_NOPE, :] = (qt[qo:qo + QK_NOPE] * q_scale).astype(BF16)
        ar = qt[qo + QK_NOPE:qo + MLA_QK] * tabt
        qt_ref[hd, QK_NOPE:QK_NOPE + QK_ROPE, :] = ((ar[:QK_ROPE] + ar[QK_ROPE:]) * q_scale).astype(BF16)
        qt_ref[hd, QK_NOPE + QK_ROPE:, :] = jnp.zeros((QK_ROPE, tm), BF16)
        k_ref[hd, :, :QK_NOPE] = k_nope[:, hd * QK_NOPE:(hd + 1) * QK_NOPE].astype(BF16)
        k_ref[hd, :, QK_NOPE:] = k_rope
        for c in range(tm // MLA_KV_CHUNK):
            vt_ref[hd, c, :V_HEAD, :] = vt[hd * V_HEAD:(hd + 1) * V_HEAD,
                                           c * MLA_KV_CHUNK:(c + 1) * MLA_KV_CHUNK].astype(BF16)
            vt_ref[hd, c, V_HEAD:, :] = ones_rows

    u_ref[...] = jax.nn.gelu(p[:, o_u:o_v]).astype(BF16)
    gv = jax.nn.gelu(p[:, o_v:o_r])
    gc = gv - jnp.mean(gv, axis=-1, keepdims=True)
    ln = gc * lax.rsqrt(jnp.mean(gc * gc, axis=-1, keepdims=True) + EPS) * gv_ref[...]
    vl_ref[...] = ln.astype(BF16)


def _even_in(x, g, w_in, g_q, g_kv, w_uq_t, w_uk, w_uv_t, g_v, tab, seq, *, tm=512):
    t, d = x.shape
    tiles_per_seq = seq // tm
    q_split = MLA_TQ // tm
    tok = lambda i: (i, 0)
    consts = [g, w_in, g_q, g_kv, w_uq_t, w_uk, w_uv_t, g_v]
    return pl.pallas_call(
        _even_in_body,
        grid=(t // tm,),
        in_specs=[pl.BlockSpec((tm, d), tok)] + [_resident(c.shape) for c in consts] + [
            pl.BlockSpec((tm, 2 * QK_ROPE), lambda i: (i % tiles_per_seq, 0)),
            pl.BlockSpec((2 * QK_ROPE, tm), lambda i: (0, i % tiles_per_seq)),
        ],
        out_specs=[
            pl.BlockSpec((MLA_HEADS, None, MLA_QK, tm), lambda i: (0, i // q_split, 0, i % q_split)),
            pl.BlockSpec((MLA_HEADS, tm, MLA_QK), lambda i: (0, i, 0)),
            pl.BlockSpec((MLA_HEADS, tm // MLA_KV_CHUNK, MLA_V_ROWS, MLA_KV_CHUNK), lambda i: (0, i, 0, 0)),
            pl.BlockSpec((tm, GM_WIDTH), tok),
            pl.BlockSpec((tm, GM_WIDTH), tok),
        ],
        out_shape=[
            jax.ShapeDtypeStruct((MLA_HEADS, t // MLA_TQ, MLA_QK, MLA_TQ), BF16),
            jax.ShapeDtypeStruct((MLA_HEADS, t, MLA_QK), BF16),
            jax.ShapeDtypeStruct((MLA_HEADS, t // MLA_KV_CHUNK, MLA_V_ROWS, MLA_KV_CHUNK), BF16),
            jax.ShapeDtypeStruct((t, GM_WIDTH), BF16),
            jax.ShapeDtypeStruct((t, GM_WIDTH), BF16),
        ],
        compiler_params=_params("parallel"),
        name="even_in",
    )(x, *consts, tab, tab.T)


def _mla_body(qt_ref, k_ref, vt_ref, o_ref, acc_ref, s_ref, *, tk):
    n_tiles, _, tq = qt_ref.shape
    sub = tk // MLA_KV_CHUNK
    n_chunks = k_ref.shape[0] // tk
    chunk_bits = n_chunks.bit_length() - 1
    assert n_chunks == 1 << chunk_bits and n_chunks % MLA_SLOTS == 0
    last = n_tiles * n_chunks - 1

    def scores(v, slot):
        tile = lax.shift_right_logical(v, jnp.int32(chunk_bits))
        off = pl.multiple_of((v & (n_chunks - 1)) * tk, tk)
        st = jnp.dot(k_ref[pl.ds(off, tk), :], qt_ref[tile], preferred_element_type=F32)
        s_ref[slot] = st
        return jnp.max(st, axis=0, keepdims=True)

    def absorb(c, slot, m_chunk, m_old):
        m_new = jnp.maximum(m_old, m_chunk)
        alpha = jnp.exp2(m_old - m_new)
        pb = jnp.exp2(s_ref[slot] - m_new).astype(BF16)
        vt = jnp.concatenate([vt_ref[c * sub + j] for j in range(sub)], axis=1)
        pv = jnp.dot(vt, pb, preferred_element_type=F32)
        acc_ref[...] = alpha * acc_ref[...] + pv
        return m_new

    def tile_step(tile, m_chunk):
        acc_ref[...] = jnp.zeros(acc_ref.shape, F32)
        m = jnp.full((1, tq), -jnp.inf, F32)
        for c in range(n_chunks):
            nxt = tile * n_chunks + c + 1
            if c == n_chunks - 1:
                nxt = jnp.minimum(nxt, last)
            m_next = scores(nxt, (c + 1) % MLA_SLOTS)
            m = absorb(c, c % MLA_SLOTS, m_chunk, m)
            m_chunk = m_next
        out = acc_ref[:V_HEAD, :] / acc_ref[V_HEAD:V_HEAD + 1, :]
        o_ref[pl.ds(pl.multiple_of(tile * tq, tq), tq), :] = out.T.astype(o_ref.dtype)
        return m_chunk

    lax.fori_loop(0, n_tiles, tile_step, scores(jnp.int32(0), 0))


def _mla(qt, k, vt, n_seq, seq, *, tk=512):
    t = n_seq * seq
    span = MLA_Q_TILES * MLA_TQ
    nq = seq // span
    assert seq % span == 0 and seq % tk == 0
    return pl.pallas_call(
        functools.partial(_mla_body, tk=tk),
        grid=(n_seq, MLA_HEADS, nq),
        in_specs=[
            pl.BlockSpec((None, MLA_Q_TILES, MLA_QK, MLA_TQ), lambda b, h, i: (h, b * nq + i, 0, 0)),
            pl.BlockSpec((None, seq, MLA_QK), lambda b, h, i: (h, b, 0)),
            pl.BlockSpec((None, seq // MLA_KV_CHUNK, MLA_V_ROWS, MLA_KV_CHUNK), lambda b, h, i: (h, b, 0, 0)),
        ],
        out_specs=pl.BlockSpec((span, V_HEAD), lambda b, h, i: (b * nq + i, h)),
        out_shape=jax.ShapeDtypeStruct((t, MLA_HEADS * V_HEAD), BF16),
        scratch_shapes=[pltpu.VMEM((MLA_V_ROWS, MLA_TQ), F32), pltpu.VMEM((MLA_SLOTS, tk, MLA_TQ), F32)],
        compiler_params=_params("parallel", "parallel", "arbitrary"),
        name="mla_attention",
    )(qt, k, vt)


def _even_out_body(x_ref, a_ref, u_ref, vl_ref, ws_ref, bs_ref, wo_ref, g_ref, o_ref, b_ref):
    tm = x_ref.shape[0]
    for c in range(tm // GM_CHUNK):
        rows = slice(c * GM_CHUNK, (c + 1) * GM_CHUNK)
        for grp in range(GM_GROUPS):
            cols = slice(grp * 128, (grp + 1) * 128)
            mixed = jnp.dot(ws_ref[grp], vl_ref[rows, cols], preferred_element_type=F32)
            mixed = mixed + bs_ref[:, grp:grp + 1]
            b_ref[rows, cols] = (u_ref[rows, cols].astype(F32) * mixed).astype(BF16)
    n_a = a_ref.shape[1]
    y = jnp.dot(a_ref[...], wo_ref[:n_a, :], preferred_element_type=F32)
    y = y + jnp.dot(b_ref[...], wo_ref[n_a:, :], preferred_element_type=F32)
    o_ref[...] = x_ref[...] + _rms(y, g_ref[...])


def _even_out(x, a, u, vl, w_s, b_s_t, w_o, g, *, tm=512):
    t, d = x.shape
    tok = lambda i: (i, 0)
    return pl.pallas_call(
        _even_out_body,
        grid=(t // tm,),
        in_specs=[
            pl.BlockSpec((tm, d), tok),
            pl.BlockSpec((tm, a.shape[1]), tok),
            pl.BlockSpec((tm, GM_WIDTH), tok),
            pl.BlockSpec((tm, GM_WIDTH), tok),
            _resident(w_s.shape), _resident(b_s_t.shape), _resident(w_o.shape), _resident(g.shape),
        ],
        out_specs=pl.BlockSpec((tm, d), tok),
        out_shape=jax.ShapeDtypeStruct((t, d), F32),
        scratch_shapes=[pltpu.VMEM((tm, GM_WIDTH), BF16)],
        compiler_params=_params("parallel"),
        name="even_out",
    )(x, a, u, vl, w_s, b_s_t, w_o, g)


def _odd_in_body(x_ref, g_ref, wqt_ref, wk_ref, wvt_ref, qt_ref, k_ref, vt_ref):
    h = _rms(x_ref[...], g_ref[...]).astype(BF16)
    qt = lax.dot_general(wqt_ref[...], h, _NT, preferred_element_type=F32)
    qt_ref[...] = (qt * (HEAD_DIM ** -0.5 * LOG2E)).astype(BF16)
    k_ref[...] = jnp.dot(h, wk_ref[...], preferred_element_type=F32).astype(BF16)
    vt_ref[...] = lax.dot_general(wvt_ref[...], h, _NT, preferred_element_type=F32).astype(BF16)


def _odd_in(x, g, w_q_t, w_k, w_v_t, *, tm=512):
    t, d = x.shape
    nq, nk = w_q_t.shape[0], w_k.shape[1]
    consts = [g, w_q_t, w_k, w_v_t]
    return pl.pallas_call(
        _odd_in_body,
        grid=(t // tm,),
        in_specs=[pl.BlockSpec((tm, d), lambda i: (i, 0))] + [_resident(c.shape) for c in consts],
        out_specs=[pl.BlockSpec((nq, tm), lambda i: (0, i)), pl.BlockSpec((tm, nk), lambda i: (i, 0)),
                   pl.BlockSpec((nk, tm), lambda i: (0, i))],
        out_shape=[jax.ShapeDtypeStruct((nq, t), BF16), jax.ShapeDtypeStruct((t, nk), BF16),
                   jax.ShapeDtypeStruct((nk, t), BF16)],
        compiler_params=_params("parallel"),
        name="odd_in",
    )(x, *consts)


def _window_body(sink_ref, x_ref, qt_ref, kp_ref, km_ref, kn_ref, vtp_ref, vtm_ref, vtn_ref, wo_ref, g_ref,
                 o_ref, k_scr, vt_scr, a_scr, a_prev, y_scr, *, seq, n_tiles):
    blk = WINDOW
    tq = qt_ref.shape[1]
    step = pl.program_id(0)

    @pl.when(step == 0)
    def _():
        a_scr[...] = jnp.zeros(a_scr.shape, a_scr.dtype)

    a_prev[...] = a_scr[...]
    d_model = wo_ref.shape[1]
    n_chains = (tq // blk) * GQA_KV_HEADS
    piece = d_model // (n_chains // 2)

    k_scr[:blk] = kp_ref[...]
    k_scr[blk:blk + tq] = km_ref[...]
    k_scr[blk + tq:] = kn_ref[...]
    vt_scr[:, :blk] = vtp_ref[...]
    vt_scr[:, blk:blk + tq] = vtm_ref[...]
    vt_scr[:, blk + tq:] = vtn_ref[...]

    j_idx = lax.broadcasted_iota(jnp.int32, (3 * blk, blk), 0)
    a_idx = lax.broadcasted_iota(jnp.int32, (3 * blk, blk), 1)
    dist = jnp.abs(j_idx - blk - a_idx)
    tile_start = (jnp.minimum(step, n_tiles - 1) % (seq // tq)) * tq

    for r in range(tq // blk):
        key_pos = tile_start + (r - 1) * blk + j_idx
        valid = (dist <= WINDOW) & (key_pos >= 0) & (key_pos < seq)
        neg_dist = jnp.where(valid, dist.astype(F32) * -LOG2E, -jnp.inf)
        q_lanes = slice(r * blk, (r + 1) * blk)
        win = slice(r * blk, (r + 3) * blk)
        for kvh in range(GQA_KV_HEADS):
            feat = slice(kvh * HEAD_DIM, (kvh + 1) * HEAD_DIM)
            heads = [kvh * GQA_GROUP + grp for grp in range(GQA_GROUP)]
            qt = jnp.concatenate([qt_ref[hd * HEAD_DIM:(hd + 1) * HEAD_DIM, q_lanes] for hd in heads], axis=1)
            st = jnp.dot(k_scr[win, feat], qt, preferred_element_type=F32)
            bias = jnp.concatenate([2.0 ** (-8.0 * (hd + 1) / GQA_HEADS) * neg_dist for hd in heads], axis=1)
            sink = jnp.concatenate([jnp.full((1, blk), sink_ref[hd] * LOG2E, F32) for hd in heads], axis=1)
            logits = st + bias
            m = jnp.maximum(jnp.max(logits, axis=0, keepdims=True), sink)
            e = jnp.exp2(logits - m)
            denom = jnp.sum(e, axis=0, keepdims=True) + jnp.exp2(sink - m)
            ot = jnp.dot(vt_scr[feat, win], e.astype(BF16), preferred_element_type=F32) / denom
            for grp, hd in enumerate(heads):
                a_scr[q_lanes, hd * HEAD_DIM:(hd + 1) * HEAD_DIM] = (
                    ot[:, grp * blk:(grp + 1) * blk].T.astype(a_scr.dtype))
            chain = r * GQA_KV_HEADS + kvh
            if chain % 2 == 1:
                cols = slice((chain // 2) * piece, (chain // 2 + 1) * piece)
                y_scr[:, cols] = jnp.dot(a_prev[...], wo_ref[:, cols], preferred_element_type=F32)
    o_ref[...] = x_ref[...] + _rms(y_scr[...], g_ref[...])


def _window(x, qt, k, vt, sink, w_o, g, n_seq, seq, *, tq=512):
    t, d = x.shape
    nq = seq // tq
    n_tiles = n_seq * nq
    r = tq // WINDOW
    nk = k.shape[1]
    blocks_per_seq = seq // WINDOW
    main = lambda s: jnp.minimum(s, n_tiles - 1)
    lagged = lambda s: jnp.maximum(s - 1, 0)
    first = lambda s: (main(s) // nq) * blocks_per_seq
    prev = lambda s: first(s) + jnp.maximum((main(s) % nq) * r - 1, 0)
    nxt = lambda s: first(s) + jnp.minimum((main(s) % nq + 1) * r, blocks_per_seq - 1)
    rows = lambda size, idx: pl.BlockSpec((size, nk), lambda s: (idx(s), 0))
    lanes = lambda size, idx: pl.BlockSpec((nk, size), lambda s: (0, idx(s)))
    tok = pl.BlockSpec((tq, d), lambda s: (lagged(s), 0))
    return pl.pallas_call(
        functools.partial(_window_body, seq=seq, n_tiles=n_tiles),
        grid=(n_tiles + 1,),
        in_specs=[pl.BlockSpec(memory_space=pltpu.SMEM), tok,
                  pl.BlockSpec((qt.shape[0], tq), lambda s: (0, main(s))),
                  rows(WINDOW, prev), rows(tq, main), rows(WINDOW, nxt),
                  lanes(WINDOW, prev), lanes(tq, main), lanes(WINDOW, nxt),
                  _resident(w_o.shape), _resident(g.shape)],
        out_specs=tok,
        out_shape=jax.ShapeDtypeStruct((t, d), F32),
        scratch_shapes=[pltpu.VMEM((tq + 2 * WINDOW, nk), BF16), pltpu.VMEM((nk, tq + 2 * WINDOW), BF16),
                        pltpu.VMEM((tq, qt.shape[0]), BF16), pltpu.VMEM((tq, qt.shape[0]), BF16),
                        pltpu.VMEM((tq, d), F32)],
        compiler_params=_params("arbitrary"),
        name="window_attention",
    )(sink, x, qt, k, k, k, vt, vt, vt, w_o, g)


def _swap_halves(w):
    half = w.shape[-1] // 2
    return jnp.concatenate([-w[..., half:], w[..., :half]], axis=-1)


def _prep_even(w_in, w_uq, w_ukv):
    o_u = Q_LORA + KV_LORA
    o_g = o_u + QK_ROPE
    k_r = w_in[:, o_u:o_g]
    w_in_x = jnp.concatenate([w_in[:, :o_u], w_in[:, o_g:], k_r, _swap_halves(k_r)], axis=-1)
    per_head = w_uq.reshape(w_uq.shape[0], MLA_HEADS, QK_NOPE + QK_ROPE)
    rope = per_head[..., QK_NOPE:]
    w_uq_x = jnp.concatenate([per_head, _swap_halves(rope)], axis=-1).reshape(w_uq.shape[0], MLA_HEADS * MLA_QK)
    kv = w_ukv.reshape(w_ukv.shape[0], MLA_HEADS, QK_NOPE + V_HEAD)
    w_uk = kv[..., :QK_NOPE].reshape(w_ukv.shape[0], MLA_HEADS * QK_NOPE)
    w_uv = kv[..., QK_NOPE:].reshape(w_ukv.shape[0], MLA_HEADS * V_HEAD)
    return w_in_x.astype(BF16), w_uq_x.T.astype(BF16), w_uk.astype(BF16), w_uv.T.astype(BF16)


def _prep_odd(w_qkv):
    nq = GQA_HEADS * HEAD_DIM
    nk = GQA_KV_HEADS * HEAD_DIM
    return (w_qkv[:, :nq].T.astype(BF16), w_qkv[:, nq:nq + nk].astype(BF16), w_qkv[:, nq + nk:].T.astype(BF16))


def _rope_table(seq):
    pos = jnp.arange(seq, dtype=F32)
    inv = ROPE_THETA ** (-jnp.arange(0, QK_ROPE, 2, dtype=F32) / QK_ROPE)
    ang = pos[:, None] * inv[None, :]
    cos, sin = jnp.cos(ang), jnp.sin(ang)
    return jnp.concatenate([cos, cos, sin, sin], axis=-1)


def _trunk(x3, w):
    n_seq, seq, d = x3.shape
    x = x3.reshape(n_seq * seq, d)
    depth = w["norm_g"].shape[0]
    for layer in range(depth):
        g = w["norm_g"][layer]
        row = lambda i: g[i][None, :]
        ffn = lambda x, n, idx: _ffn(x, row(n), row(n + 1), w["gate"], w["up"], w["down"], layer, idx)
        x = ffn(x, 0, 0)
        i = layer // 2
        if layer % 2 == 0:
            w_in, w_uq_t, w_uk, w_uv_t = w["even"][i]
            qt, k, vt, u, vl = _even_in(x, row(2), w_in, w["g_q"][i][None, :], w["g_kv"][i][None, :],
                                        w_uq_t, w_uk, w_uv_t, w["g_v"][i][None, :], _rope_table(seq), seq)
            a = _mla(qt, k, vt, n_seq, seq)
            x = _even_out(x, a, u, vl, w["w_s"][i], w["b_s_t"][i], w["even_o"][i], row(3))
        else:
            qt, k, vt = _odd_in(x, row(2), *w["odd"][i])
            x = _window(x, qt, k, vt, w["sink"][i], w["odd_o"][i], row(3), n_seq, seq)
        x = ffn(x, 4, 1)
    return x.reshape(n_seq, seq, d)


def kernel(x_prompt, x_sample, norm_g, ffn_w_gate, ffn_w_up, ffn_w_down, even_w_in, mla_g_q, mla_g_kv, mla_w_uq, mla_w_ukv, gm_g_v, gm_w_s, gm_b_s, even_w_o, odd_w_qkv, odd_sink, odd_w_o):
    even = [_prep_even(even_w_in[i], mla_w_uq[i], mla_w_ukv[i]) for i in range(even_w_in.shape[0])]
    w = {
        "norm_g": norm_g,
        "gate": ffn_w_gate.astype(BF16), "up": ffn_w_up.astype(BF16), "down": ffn_w_down.astype(BF16),
        "even": even, "g_q": mla_g_q, "g_kv": mla_g_kv, "g_v": gm_g_v,
        "w_s": gm_w_s.astype(BF16), "b_s_t": jnp.swapaxes(gm_b_s, 1, 2), "even_o": even_w_o.astype(BF16),
        "odd": [_prep_odd(odd_w_qkv[i]) for i in range(odd_w_qkv.shape[0])],
        "sink": odd_sink, "odd_o": odd_w_o.astype(BF16),
    }
    return (_trunk(x_prompt, w), _trunk(x_sample, w))
```

```python
import functools

import jax
import jax.numpy as jnp
from jax import lax
from jax.experimental import pallas as pl
from jax.experimental.pallas import tpu as pltpu

F32 = jnp.float32
BF16 = jnp.bfloat16

EPS = 1e-6
LOG2E = 1.4426950408889634

MLA_HEADS = 8
Q_LORA = 768
KV_LORA = 512
QK_NOPE = 128
QK_ROPE = 64
V_HEAD = 128
ROPE_THETA = 10000.0
MLA_QK = QK_NOPE + 2 * QK_ROPE
MLA_KV_CHUNK = 256
MLA_TQ = 1024
MLA_Q_TILES = 8
MLA_SLOTS = 4
MLA_V_ROWS = V_HEAD + 16

GM_GROUPS = 8
GM_CHUNK = 128
GM_WIDTH = GM_GROUPS * 128

GQA_HEADS = 16
GQA_KV_HEADS = 4
GQA_GROUP = GQA_HEADS // GQA_KV_HEADS
HEAD_DIM = 128
WINDOW = 128

FFN_ROW_SPLIT = 2
FFN_CHUNK = 512

VMEM_LIMIT_BYTES = 56 * 1024 * 1024


def _params(*semantics):
    return pltpu.CompilerParams(dimension_semantics=semantics, vmem_limit_bytes=VMEM_LIMIT_BYTES)


def _resident(shape):
    zeros = (0,) * len(shape)
    return pl.BlockSpec(shape, lambda *_: zeros, pipeline_mode=pl.Buffered(1))


def _rms(x, g):
    return x * lax.rsqrt(jnp.mean(x * x, axis=-1, keepdims=True) + EPS) * g


def _ffn_body(x_ref, gin_ref, gout_ref, wg_ref, wu_ref, wd_ref, o_ref, xn_ref):
    acc_ref = o_ref
    j = pl.program_id(1)
    last = pl.num_programs(1) - 1
    tr = x_ref.shape[0] // FFN_ROW_SPLIT
    row_blocks = [slice(r * tr, (r + 1) * tr) for r in range(FFN_ROW_SPLIT)]

    def chunk(rows, first):
        xn = xn_ref[rows, :]
        gate = jnp.dot(xn, wg_ref[...], preferred_element_type=F32)
        up = jnp.dot(xn, wu_ref[...], preferred_element_type=F32)
        h = (gate * jax.nn.sigmoid(gate) * up).astype(BF16)
        part = jnp.dot(h, wd_ref[...], preferred_element_type=F32)
        if first:
            acc_ref[rows, :] = part
        else:
            acc_ref[rows, :] += part

    @pl.when(j == 0)
    def _():
        for rows in row_blocks:
            xn_ref[rows, :] = _rms(x_ref[rows, :], gin_ref[...]).astype(BF16)
            chunk(rows, True)

    @pl.when((j > 0) & (j < last))
    def _():
        for rows in row_blocks:
            chunk(rows, False)

    @pl.when(j == last)
    def _():
        for rows in row_blocks:
            chunk(rows, False)
            o_ref[rows, :] = x_ref[rows, :] + 0.5 * _rms(acc_ref[rows, :], gout_ref[...])


def _ffn(x, g_in, g_out, w_gate, w_up, w_down, layer, idx, *, tm=1024, tf=FFN_CHUNK):
    t, d = x.shape
    n_chunks = w_gate.shape[-1] // tf
    assert n_chunks >= 3, "first / middle / last d_ff chunks are separate code paths"
    return pl.pallas_call(
        _ffn_body,
        grid=(t // tm, n_chunks),
        in_specs=[
            pl.BlockSpec((tm, d), lambda i, j: (i, 0)),
            pl.BlockSpec((1, d), lambda i, j: (0, 0)),
            pl.BlockSpec((1, d), lambda i, j: (0, 0)),
            pl.BlockSpec((None, None, d, tf), lambda i, j: (layer, idx, 0, j)),
            pl.BlockSpec((None, None, d, tf), lambda i, j: (layer, idx, 0, j)),
            pl.BlockSpec((None, None, tf, d), lambda i, j: (layer, idx, j, 0)),
        ],
        out_specs=pl.BlockSpec((tm, d), lambda i, j: (i, 0)),
        out_shape=jax.ShapeDtypeStruct((t, d), F32),
        scratch_shapes=[pltpu.VMEM((tm, d), BF16)],
        compiler_params=_params("parallel", "arbitrary"),
        name="ffn",
    )(x, g_in, g_out, w_gate, w_up, w_down)


_NT = (((1,), (1,)), ((), ()))


def _even_in_body(x_ref, g_ref, win_ref, gq_ref, gkv_ref, wuqt_ref, wuk_ref, wuvt_ref, gv_ref, tab_ref,
                  tabt_ref, qt_ref, k_ref, vt_ref, u_ref, vl_ref):
    h = _rms(x_ref[...], g_ref[...]).astype(BF16)
    p = jnp.dot(h, win_ref[...], preferred_element_type=F32)
    o_kv = Q_LORA
    o_u = Q_LORA + KV_LORA
    o_v = o_u + GM_WIDTH
    o_r = o_v + GM_WIDTH

    c_q = _rms(p[:, :o_kv], gq_ref[...]).astype(BF16)
    c_kv = _rms(p[:, o_kv:o_u], gkv_ref[...]).astype(BF16)
    qt = lax.dot_general(wuqt_ref[...], c_q, _NT, preferred_element_type=F32)
    vt = lax.dot_general(wuvt_ref[...], c_kv, _NT, preferred_element_type=F32)
    k_nope = jnp.dot(c_kv, wuk_ref[...], preferred_element_type=F32)

    a = p[:, o_r:o_r + 2 * QK_ROPE] * tab_ref[...]
    k_rope = (a + pltpu.roll(a, QK_ROPE, axis=1)).astype(BF16)
    tabt = tabt_ref[...]
    q_scale = (QK_NOPE + QK_ROPE) ** -0.5 * LOG2E
    tm = x_ref.shape[0]
    row = lax.broadcasted_iota(jnp.int32, (MLA_V_ROWS - V_HEAD, MLA_KV_CHUNK), 0)
    ones_rows = jnp.where(row == 0, 1.0, 0.0).astype(BF16)
    for hd in range(MLA_HEADS):
        qo = hd * MLA_QK
        qt_ref[hd, :QK_NOPE, :] = (qt[qo:qo + QK_NOPE] * q_scale).astype(BF16)
        ar = qt[qo + QK_NOPE:qo + MLA_QK] * tabt
        qt_ref[hd, QK_NOPE:QK_NOPE + QK_ROPE, :] = ((ar[:QK_ROPE] + ar[QK_ROPE:]) * q_scale).astype(BF16)
        qt_ref[hd, QK_NOPE + QK_ROPE:, :] = jnp.zeros((QK_ROPE, tm), BF16)
        k_ref[hd, :, :QK_NOPE] = k_nope[:, hd * QK_NOPE:(hd + 1) * QK_NOPE].astype(BF16)
        k_ref[hd, :, QK_NOPE:] = k_rope
        for c in range(tm // MLA_KV_CHUNK):
            vt_ref[hd, c, :V_HEAD, :] = vt[hd * V_HEAD:(hd + 1) * V_HEAD,
                                           c * MLA_KV_CHUNK:(c + 1) * MLA_KV_CHUNK].astype(BF16)
            vt_ref[hd, c, V_HEAD:, :] = ones_rows

    u_ref[...] = jax.nn.gelu(p[:, o_u:o_v]).astype(BF16)
    gv = jax.nn.gelu(p[:, o_v:o_r])
    gc = gv - jnp.mean(gv, axis=-1, keepdims=True)
    ln = gc * lax.rsqrt(jnp.mean(gc * gc, axis=-1, keepdims=True) + EPS) * gv_ref[...]
    vl_ref[...] = ln.astype(BF16)


def _even_in(x, g, w_in, g_q, g_kv, w_uq_t, w_uk, w_uv_t, g_v, tab, seq, *, tm=512):
    t, d = x.shape
    tiles_per_seq = seq // tm
    q_split = MLA_TQ // tm
    tok = lambda i: (i, 0)
    consts = [g, w_in, g_q, g_kv, w_uq_t, w_uk, w_uv_t, g_v]
    return pl.pallas_call(
        _even_in_body,
        grid=(t // tm,),
        in_specs=[pl.BlockSpec((tm, d), tok)] + [_resident(c.shape) for c in consts] + [
            pl.BlockSpec((tm, 2 * QK_ROPE), lambda i: (i % tiles_per_seq, 0)),
            pl.BlockSpec((2 * QK_ROPE, tm), lambda i: (0, i % tiles_per_seq)),
        ],
        out_specs=[
            pl.BlockSpec((MLA_HEADS, None, MLA_QK, tm), lambda i: (0, i // q_split, 0, i % q_split)),
            pl.BlockSpec((MLA_HEADS, tm, MLA_QK), lambda i: (0, i, 0)),
            pl.BlockSpec((MLA_HEADS, tm // MLA_KV_CHUNK, MLA_V_ROWS, MLA_KV_CHUNK), lambda i: (0, i, 0, 0)),
            pl.BlockSpec((tm, GM_WIDTH), tok),
            pl.BlockSpec((tm, GM_WIDTH), tok),
        ],
        out_shape=[
            jax.ShapeDtypeStruct((MLA_HEADS, t // MLA_TQ, MLA_QK, MLA_TQ), BF16),
            jax.ShapeDtypeStruct((MLA_HEADS, t, MLA_QK), BF16),
            jax.ShapeDtypeStruct((MLA_HEADS, t // MLA_KV_CHUNK, MLA_V_ROWS, MLA_KV_CHUNK), BF16),
            jax.ShapeDtypeStruct((t, GM_WIDTH), BF16),
            jax.ShapeDtypeStruct((t, GM_WIDTH), BF16),
        ],
        compiler_params=_params("parallel"),
        name="even_in",
    )(x, *consts, tab, tab.T)


def _mla_body(qt_ref, k_ref, vt_ref, o_ref, acc_ref, s_ref, *, tk):
    n_tiles, _, tq = qt_ref.shape
    sub = tk // MLA_KV_CHUNK
    n_chunks = k_ref.shape[0] // tk
    chunk_bits = n_chunks.bit_length() - 1
    assert n_chunks == 1 << chunk_bits and n_chunks % MLA_SLOTS == 0
    last = n_tiles * n_chunks - 1

    def scores(v, slot):
        tile = lax.shift_right_logical(v, jnp.int32(chunk_bits))
        off = pl.multiple_of((v & (n_chunks - 1)) * tk, tk)
        st = jnp.dot(k_ref[pl.ds(off, tk), :], qt_ref[tile], preferred_element_type=F32)
        s_ref[slot] = st
        return jnp.max(st, axis=0, keepdims=True)

    def absorb(c, slot, m_chunk, m_old):
        m_new = jnp.maximum(m_old, m_chunk)
        alpha = jnp.exp2(m_old - m_new)
        pb = jnp.exp2(s_ref[slot] - m_new).astype(BF16)
        vt = jnp.concatenate([vt_ref[c * sub + j] for j in range(sub)], axis=1)
        pv = jnp.dot(vt, pb, preferred_element_type=F32)
        acc_ref[...] = alpha * acc_ref[...] + pv
        return m_new

    def tile_step(tile, m_chunk):
        acc_ref[...] = jnp.zeros(acc_ref.shape, F32)
        m = jnp.full((1, tq), -jnp.inf, F32)
        for c in range(n_chunks):
            nxt = tile * n_chunks + c + 1
            if c == n_chunks - 1:
                nxt = jnp.minimum(nxt, last)
            m_next = scores(nxt, (c + 1) % MLA_SLOTS)
            m = absorb(c, c % MLA_SLOTS, m_chunk, m)
            m_chunk = m_next
        out = acc_ref[:V_HEAD, :] / acc_ref[V_HEAD:V_HEAD + 1, :]
        o_ref[pl.ds(pl.multiple_of(tile * tq, tq), tq), :] = out.T.astype(o_ref.dtype)
        return m_chunk

    lax.fori_loop(0, n_tiles, tile_step, scores(jnp.int32(0), 0))


def _mla(qt, k, vt, n_seq, seq, *, tk=512):
    t = n_seq * seq
    span = MLA_Q_TILES * MLA_TQ
    nq = seq // span
    assert seq % span == 0 and seq % tk == 0
    return pl.pallas_call(
        functools.partial(_mla_body, tk=tk),
        grid=(n_seq, MLA_HEADS, nq),
        in_specs=[
            pl.BlockSpec((None, MLA_Q_TILES, MLA_QK, MLA_TQ), lambda b, h, i: (h, b * nq + i, 0, 0)),
            pl.BlockSpec((None, seq, MLA_QK), lambda b, h, i: (h, b, 0)),
            pl.BlockSpec((None, seq // MLA_KV_CHUNK, MLA_V_ROWS, MLA_KV_CHUNK), lambda b, h, i: (h, b, 0, 0)),
        ],
        out_specs=pl.BlockSpec((span, V_HEAD), lambda b, h, i: (b * nq + i, h)),
        out_shape=jax.ShapeDtypeStruct((t, MLA_HEADS * V_HEAD), BF16),
        scratch_shapes=[pltpu.VMEM((MLA_V_ROWS, MLA_TQ), F32), pltpu.VMEM((MLA_SLOTS, tk, MLA_TQ), F32)],
        compiler_params=_params("parallel", "parallel", "arbitrary"),
        name="mla_attention",
    )(qt, k, vt)


def _even_out_body(x_ref, a_ref, u_ref, vl_ref, ws_ref, bs_ref, wo_ref, g_ref, o_ref, b_ref):
    tm = x_ref.shape[0]
    for c in range(tm // GM_CHUNK):
        rows = slice(c * GM_CHUNK, (c + 1) * GM_CHUNK)
        for grp in range(GM_GROUPS):
            cols = slice(grp * 128, (grp + 1) * 128)
            mixed = jnp.dot(ws_ref[grp], vl_ref[rows, cols], preferred_element_type=F32)
            mixed = mixed + bs_ref[:, grp:grp + 1]
            b_ref[rows, cols] = (u_ref[rows, cols].astype(F32) * mixed).astype(BF16)
    n_a = a_ref.shape[1]
    y = jnp.dot(a_ref[...], wo_ref[:n_a, :], preferred_element_type=F32)
    y = y + jnp.dot(b_ref[...], wo_ref[n_a:, :], preferred_element_type=F32)
    o_ref[...] = x_ref[...] + _rms(y, g_ref[...])


def _even_out(x, a, u, vl, w_s, b_s_t, w_o, g, *, tm=512):
    t, d = x.shape
    tok = lambda i: (i, 0)
    return pl.pallas_call(
        _even_out_body,
        grid=(t // tm,),
        in_specs=[
            pl.BlockSpec((tm, d), tok),
            pl.BlockSpec((tm, a.shape[1]), tok),
            pl.BlockSpec((tm, GM_WIDTH), tok),
            pl.BlockSpec((tm, GM_WIDTH), tok),
            _resident(w_s.shape), _resident(b_s_t.shape), _resident(w_o.shape), _resident(g.shape),
        ],
        out_specs=pl.BlockSpec((tm, d), tok),
        out_shape=jax.ShapeDtypeStruct((t, d), F32),
        scratch_shapes=[pltpu.VMEM((tm, GM_WIDTH), BF16)],
        compiler_params=_params("parallel"),
        name="even_out",
    )(x, a, u, vl, w_s, b_s_t, w_o, g)


def _odd_in_body(x_ref, g_ref, wqt_ref, wk_ref, wvt_ref, qt_ref, k_ref, vt_ref):
    h = _rms(x_ref[...], g_ref[...]).astype(BF16)
    qt = lax.dot_general(wqt_ref[...], h, _NT, preferred_element_type=F32)
    qt_ref[...] = (qt * (HEAD_DIM ** -0.5 * LOG2E)).astype(BF16)
    k_ref[...] = jnp.dot(h, wk_ref[...], preferred_element_type=F32).astype(BF16)
    vt_ref[...] = lax.dot_general(wvt_ref[...], h, _NT, preferred_element_type=F32).astype(BF16)


def _odd_in(x, g, w_q_t, w_k, w_v_t, *, tm=512):
    t, d = x.shape
    nq, nk = w_q_t.shape[0], w_k.shape[1]
    consts = [g, w_q_t, w_k, w_v_t]
    return pl.pallas_call(
        _odd_in_body,
        grid=(t // tm,),
        in_specs=[pl.BlockSpec((tm, d), lambda i: (i, 0))] + [_resident(c.shape) for c in consts],
        out_specs=[pl.BlockSpec((nq, tm), lambda i: (0, i)), pl.BlockSpec((tm, nk), lambda i: (i, 0)),
                   pl.BlockSpec((nk, tm), lambda i: (0, i))],
        out_shape=[jax.ShapeDtypeStruct((nq, t), BF16), jax.ShapeDtypeStruct((t, nk), BF16),
                   jax.ShapeDtypeStruct((nk, t), BF16)],
        compiler_params=_params("parallel"),
        name="odd_in",
    )(x, *consts)


def _window_body(sink_ref, x_ref, qt_ref, kp_ref, km_ref, kn_ref, vtp_ref, vtm_ref, vtn_ref, wo_ref, g_ref,
                 o_ref, k_scr, vt_scr, a_scr, a_prev, y_scr, *, seq, n_tiles):
    blk = WINDOW
    tq = qt_ref.shape[1]
    step = pl.program_id(0)

    @pl.when(step == 0)
    def _():
        a_scr[...] = jnp.zeros(a_scr.shape, a_scr.dtype)

    a_prev[...] = a_scr[...]
    d_model = wo_ref.shape[1]
    n_chains = (tq // blk) * GQA_KV_HEADS
    piece = d_model // (n_chains // 2)

    k_scr[:blk] = kp_ref[...]
    k_scr[blk:blk + tq] = km_ref[...]
    k_scr[blk + tq:] = kn_ref[...]
    vt_scr[:, :blk] = vtp_ref[...]
    vt_scr[:, blk:blk + tq] = vtm_ref[...]
    vt_scr[:, blk + tq:] = vtn_ref[...]

    j_idx = lax.broadcasted_iota(jnp.int32, (3 * blk, blk), 0)
    a_idx = lax.broadcasted_iota(jnp.int32, (3 * blk, blk), 1)
    dist = jnp.abs(j_idx - blk - a_idx)
    tile_start = (jnp.minimum(step, n_tiles - 1) % (seq // tq)) * tq

    for r in range(tq // blk):
        key_pos = tile_start + (r - 1) * blk + j_idx
        valid = (dist <= WINDOW) & (key_pos >= 0) & (key_pos < seq)
        neg_dist = jnp.where(valid, dist.astype(F32) * -LOG2E, -jnp.inf)
        q_lanes = slice(r * blk, (r + 1) * blk)
        win = slice(r * blk, (r + 3) * blk)
        for kvh in range(GQA_KV_HEADS):
            feat = slice(kvh * HEAD_DIM, (kvh + 1) * HEAD_DIM)
            heads = [kvh * GQA_GROUP + grp for grp in range(GQA_GROUP)]
            qt = jnp.concatenate([qt_ref[hd * HEAD_DIM:(hd + 1) * HEAD_DIM, q_lanes] for hd in heads], axis=1)
            st = jnp.dot(k_scr[win, feat], qt, preferred_element_type=F32)
            bias = jnp.concatenate([2.0 ** (-8.0 * (hd + 1) / GQA_HEADS) * neg_dist for hd in heads], axis=1)
            sink = jnp.concatenate([jnp.full((1, blk), sink_ref[hd] * LOG2E, F32) for hd in heads], axis=1)
            logits = st + bias
            m = jnp.maximum(jnp.max(logits, axis=0, keepdims=True), sink)
            e = jnp.exp2(logits - m)
            denom = jnp.sum(e, axis=0, keepdims=True) + jnp.exp2(sink - m)
            ot = jnp.dot(vt_scr[feat, win], e.astype(BF16), preferred_element_type=F32) / denom
            for grp, hd in enumerate(heads):
                a_scr[q_lanes, hd * HEAD_DIM:(hd + 1) * HEAD_DIM] = (
                    ot[:, grp * blk:(grp + 1) * blk].T.astype(a_scr.dtype))
            chain = r * GQA_KV_HEADS + kvh
            if chain % 2 == 0:
                cols = slice((chain // 2) * piece, (chain // 2 + 1) * piece)
                y_scr[:, cols] = jnp.dot(a_prev[...], wo_ref[:, cols], preferred_element_type=F32)
    o_ref[...] = x_ref[...] + _rms(y_scr[...], g_ref[...])


def _window(x, qt, k, vt, sink, w_o, g, n_seq, seq, *, tq=512):
    t, d = x.shape
    nq = seq // tq
    n_tiles = n_seq * nq
    r = tq // WINDOW
    nk = k.shape[1]
    blocks_per_seq = seq // WINDOW
    main = lambda s: jnp.minimum(s, n_tiles - 1)
    lagged = lambda s: jnp.maximum(s - 1, 0)
    first = lambda s: (main(s) // nq) * blocks_per_seq
    prev = lambda s: first(s) + jnp.maximum((main(s) % nq) * r - 1, 0)
    nxt = lambda s: first(s) + jnp.minimum((main(s) % nq + 1) * r, blocks_per_seq - 1)
    rows = lambda size, idx: pl.BlockSpec((size, nk), lambda s: (idx(s), 0))
    lanes = lambda size, idx: pl.BlockSpec((nk, size), lambda s: (0, idx(s)))
    tok = pl.BlockSpec((tq, d), lambda s: (lagged(s), 0))
    return pl.pallas_call(
        functools.partial(_window_body, seq=seq, n_tiles=n_tiles),
        grid=(n_tiles + 1,),
        in_specs=[pl.BlockSpec(memory_space=pltpu.SMEM), tok,
                  pl.BlockSpec((qt.shape[0], tq), lambda s: (0, main(s))),
                  rows(WINDOW, prev), rows(tq, main), rows(WINDOW, nxt),
                  lanes(WINDOW, prev), lanes(tq, main), lanes(WINDOW, nxt),
                  _resident(w_o.shape), _resident(g.shape)],
        out_specs=tok,
        out_shape=jax.ShapeDtypeStruct((t, d), F32),
        scratch_shapes=[pltpu.VMEM((tq + 2 * WINDOW, nk), BF16), pltpu.VMEM((nk, tq + 2 * WINDOW), BF16),
                        pltpu.VMEM((tq, qt.shape[0]), BF16), pltpu.VMEM((tq, qt.shape[0]), BF16),
                        pltpu.VMEM((tq, d), F32)],
        compiler_params=_params("arbitrary"),
        name="window_attention",
    )(sink, x, qt, k, k, k, vt, vt, vt, w_o, g)


def _swap_halves(w):
    half = w.shape[-1] // 2
    return jnp.concatenate([-w[..., half:], w[..., :half]], axis=-1)


def _prep_even(w_in, w_uq, w_ukv):
    o_u = Q_LORA + KV_LORA
    o_g = o_u + QK_ROPE
    k_r = w_in[:, o_u:o_g]
    w_in_x = jnp.concatenate([w_in[:, :o_u], w_in[:, o_g:], k_r, _swap_halves(k_r)], axis=-1)
    per_head = w_uq.reshape(w_uq.shape[0], MLA_HEADS, QK_NOPE + QK_ROPE)
    rope = per_head[..., QK_NOPE:]
    w_uq_x = jnp.concatenate([per_head, _swap_halves(rope)], axis=-1).reshape(w_uq.shape[0], MLA_HEADS * MLA_QK)
    kv = w_ukv.reshape(w_ukv.shape[0], MLA_HEADS, QK_NOPE + V_HEAD)
    w_uk = kv[..., :QK_NOPE].reshape(w_ukv.shape[0], MLA_HEADS * QK_NOPE)
    w_uv = kv[..., QK_NOPE:].reshape(w_ukv.shape[0], MLA_HEADS * V_HEAD)
    return w_in_x.astype(BF16), w_uq_x.T.astype(BF16), w_uk.astype(BF16), w_uv.T.astype(BF16)


def _prep_odd(w_qkv):
    nq = GQA_HEADS * HEAD_DIM
    nk = GQA_KV_HEADS * HEAD_DIM
    return (w_qkv[:, :nq].T.astype(BF16), w_qkv[:, nq:nq + nk].astype(BF16), w_qkv[:, nq + nk:].T.astype(BF16))


def _rope_table(seq):
    pos = jnp.arange(seq, dtype=F32)
    inv = ROPE_THETA ** (-jnp.arange(0, QK_ROPE, 2, dtype=F32) / QK_ROPE)
    ang = pos[:, None] * inv[None, :]
    cos, sin = jnp.cos(ang), jnp.sin(ang)
    return jnp.concatenate([cos, cos, sin, sin], axis=-1)


def _trunk(x3, w):
    n_seq, seq, d = x3.shape
    x = x3.reshape(n_seq * seq, d)
    depth = w["norm_g"].shape[0]
    for layer in range(depth):
        g = w["norm_g"][layer]
        row = lambda i: g[i][None, :]
        ffn = lambda x, n, idx: _ffn(x, row(n), row(n + 1), w["gate"], w["up"], w["down"], layer, idx)
        x = ffn(x, 0, 0)
        i = layer // 2
        if layer % 2 == 0:
            w_in, w_uq_t, w_uk, w_uv_t = w["even"][i]
            qt, k, vt, u, vl = _even_in(x, row(2), w_in, w["g_q"][i][None, :], w["g_kv"][i][None, :],
                                        w_uq_t, w_uk, w_uv_t, w["g_v"][i][None, :], _rope_table(seq), seq)
            a = _mla(qt, k, vt, n_seq, seq)
            x = _even_out(x, a, u, vl, w["w_s"][i], w["b_s_t"][i], w["even_o"][i], row(3))
        else:
            qt, k, vt = _odd_in(x, row(2), *w["odd"][i])
            x = _window(x, qt, k, vt, w["sink"][i], w["odd_o"][i], row(3), n_seq, seq)
        x = ffn(x, 4, 1)
    return x.reshape(n_seq, seq, d)


def kernel(x_prompt, x_sample, norm_g, ffn_w_gate, ffn_w_up, ffn_w_down, even_w_in, mla_g_q, mla_g_kv, mla_w_uq, mla_w_ukv, gm_g_v, gm_w_s, gm_b_s, even_w_o, odd_w_qkv, odd_sink, odd_w_o):
    even = [_prep_even(even_w_in[i], mla_w_uq[i], mla_w_ukv[i]) for i in range(even_w_in.shape[0])]
    w = {
        "norm_g": norm_g,
        "gate": ffn_w_gate.astype(BF16), "up": ffn_w_up.astype(BF16), "down": ffn_w_down.astype(BF16),
        "even": even, "g_q": mla_g_q, "g_kv": mla_g_kv, "g_v": gm_g_v,
        "w_s": gm_w_s.astype(BF16), "b_s_t": jnp.swapaxes(gm_b_s, 1, 2), "even_o": even_w_o.astype(BF16),
        "odd": [_prep_odd(odd_w_qkv[i]) for i in range(odd_w_qkv.shape[0])],
        "sink": odd_sink, "odd_o": odd_w_o.astype(BF16),
    }
    return (_trunk(x_prompt, w), _trunk(x_sample, w))
```

```python
import functools

import jax
import jax.numpy as jnp
from jax import lax
from jax.experimental import pallas as pl
from jax.experimental.pallas import tpu as pltpu

F32 = jnp.float32
BF16 = jnp.bfloat16

EPS = 1e-6
LOG2E = 1.4426950408889634

MLA_HEADS = 8
Q_LORA = 768
KV_LORA = 512
QK_NOPE = 128
QK_ROPE = 64
V_HEAD = 128
ROPE_THETA = 10000.0
MLA_QK = QK_NOPE + 2 * QK_ROPE
MLA_KV_CHUNK = 256
MLA_TQ = 1024
MLA_Q_TILES = 8
MLA_SLOTS = 4
MLA_V_ROWS = V_HEAD + 16

GM_GROUPS = 8
GM_CHUNK = 128
GM_WIDTH = GM_GROUPS * 128

GQA_HEADS = 16
GQA_KV_HEADS = 4
GQA_GROUP = GQA_HEADS // GQA_KV_HEADS
HEAD_DIM = 128
WINDOW = 128

FFN_ROW_SPLIT = 2
FFN_CHUNK = 512

VMEM_LIMIT_BYTES = 56 * 1024 * 1024


def _params(*semantics):
    return pltpu.CompilerParams(dimension_semantics=semantics, vmem_limit_bytes=VMEM_LIMIT_BYTES)


def _resident(shape):
    zeros = (0,) * len(shape)
    return pl.BlockSpec(shape, lambda *_: zeros, pipeline_mode=pl.Buffered(1))


def _rms(x, g):
    return x * lax.rsqrt(jnp.mean(x * x, axis=-1, keepdims=True) + EPS) * g


def _ffn_body(x_ref, gin_ref, gout_ref, wg_ref, wu_ref, wd_ref, o_ref, xn_ref):
    acc_ref = o_ref
    j = pl.program_id(1)
    last = pl.num_programs(1) - 1
    tr = x_ref.shape[0] // FFN_ROW_SPLIT
    row_blocks = [slice(r * tr, (r + 1) * tr) for r in range(FFN_ROW_SPLIT)]

    def chunk(rows, first):
        xn = xn_ref[rows, :]
        gate = jnp.dot(xn, wg_ref[...], preferred_element_type=F32)
        up = jnp.dot(xn, wu_ref[...], preferred_element_type=F32)
        h = (gate * jax.nn.sigmoid(gate) * up).astype(BF16)
        part = jnp.dot(h, wd_ref[...], preferred_element_type=F32)
        if first:
            acc_ref[rows, :] = part
        else:
            acc_ref[rows, :] += part

    @pl.when(j == 0)
    def _():
        for rows in row_blocks:
            xn_ref[rows, :] = _rms(x_ref[rows, :], gin_ref[...]).astype(BF16)
            chunk(rows, True)

    @pl.when((j > 0) & (j < last))
    def _():
        for rows in row_blocks:
            chunk(rows, False)

    @pl.when(j == last)
    def _():
        for rows in row_blocks:
            chunk(rows, False)
            o_ref[rows, :] = x_ref[rows, :] + 0.5 * _rms(acc_ref[rows, :], gout_ref[...])


def _ffn(x, g_in, g_out, w_gate, w_up, w_down, layer, idx, *, tm=1024, tf=FFN_CHUNK):
    t, d = x.shape
    n_chunks = w_gate.shape[-1] // tf
    assert n_chunks >= 3, "first / middle / last d_ff chunks are separate code paths"
    return pl.pallas_call(
        _ffn_body,
        grid=(t // tm, n_chunks),
        in_specs=[
            pl.BlockSpec((tm, d), lambda i, j: (i, 0)),
            pl.BlockSpec((1, d), lambda i, j: (0, 0)),
            pl.BlockSpec((1, d), lambda i, j: (0, 0)),
            pl.BlockSpec((None, None, d, tf), lambda i, j: (layer, idx, 0, j)),
            pl.BlockSpec((None, None, d, tf), lambda i, j: (layer, idx, 0, j)),
            pl.BlockSpec((None, None, tf, d), lambda i, j: (layer, idx, j, 0)),
        ],
        out_specs=pl.BlockSpec((tm, d), lambda i, j: (i, 0)),
        out_shape=jax.ShapeDtypeStruct((t, d), F32),
        scratch_shapes=[pltpu.VMEM((tm, d), BF16)],
        compiler_params=_params("parallel", "arbitrary"),
        name="ffn",
    )(x, g_in, g_out, w_gate, w_up, w_down)


_NT = (((1,), (1,)), ((), ()))


def _even_in_body(x_ref, g_ref, win_ref, gq_ref, gkv_ref, wuqt_ref, wuk_ref, wuvt_ref, gv_ref, tab_ref,
                  tabt_ref, qt_ref, k_ref, vt_ref, u_ref, vl_ref):
    h = _rms(x_ref[...], g_ref[...]).astype(BF16)
    p = jnp.dot(h, win_ref[...], preferred_element_type=F32)
    o_kv = Q_LORA
    o_u = Q_LORA + KV_LORA
    o_v = o_u + GM_WIDTH
    o_r = o_v + GM_WIDTH

    c_q = _rms(p[:, :o_kv], gq_ref[...]).astype(BF16)
    c_kv = _rms(p[:, o_kv:o_u], gkv_ref[...]).astype(BF16)
    qt = lax.dot_general(wuqt_ref[...], c_q, _NT, preferred_element_type=F32)
    vt = lax.dot_general(wuvt_ref[...], c_kv, _NT, preferred_element_type=F32)
    k_nope = jnp.dot(c_kv, wuk_ref[...], preferred_element_type=F32)

    a = p[:, o_r:o_r + 2 * QK_ROPE] * tab_ref[...]
    k_rope = (a + pltpu.roll(a, QK_ROPE, axis=1)).astype(BF16)
    tabt = tabt_ref[...]
    q_scale = (QK_NOPE + QK_ROPE) ** -0.5 * LOG2E
    tm = x_ref.shape[0]
    row = lax.broadcasted_iota(jnp.int32, (MLA_V_ROWS - V_HEAD, MLA_KV_CHUNK), 0)
    ones_rows = jnp.where(row == 0, 1.0, 0.0).astype(BF16)
    for hd in range(MLA_HEADS):
        qo = hd * MLA_QK
        qt_ref[hd, :QK_NOPE, :] = (qt[qo:qo + QK_NOPE] * q_scale).astype(BF16)
        ar = qt[qo + QK_NOPE:qo + MLA_QK] * tabt
        qt_ref[hd, QK_NOPE:QK_NOPE + QK_ROPE, :] = ((ar[:QK_ROPE] + ar[QK_ROPE:]) * q_scale).astype(BF16)
        qt_ref[hd, QK_NOPE + QK_ROPE:, :] = jnp.zeros((QK_ROPE, tm), BF16)
        k_ref[hd, :, :QK_NOPE] = k_nope[:, hd * QK_NOPE:(hd + 1) * QK_NOPE].astype(BF16)
        k_ref[hd, :, QK_NOPE:] = k_rope
        for c in range(tm // MLA_KV_CHUNK):
            vt_ref[hd, c, :V_HEAD, :] = vt[hd * V_HEAD:(hd + 1) * V_HEAD,
                                           c * MLA_KV_CHUNK:(c + 1) * MLA_KV_CHUNK].astype(BF16)
            vt_ref[hd, c, V_HEAD:, :] = ones_rows

    u_ref[...] = jax.nn.gelu(p[:, o_u:o_v]).astype(BF16)
    gv = jax.nn.gelu(p[:, o_v:o_r])
    gc = gv - jnp.mean(gv, axis=-1, keepdims=True)
    ln = gc * lax.rsqrt(jnp.mean(gc * gc, axis=-1, keepdims=True) + EPS) * gv_ref[...]
    vl_ref[...] = ln.astype(BF16)


def _even_in(x, g, w_in, g_q, g_kv, w_uq_t, w_uk, w_uv_t, g_v, tab, seq, *, tm=512):
    t, d = x.shape
    tiles_per_seq = seq // tm
    q_split = MLA_TQ // tm
    tok = lambda i: (i, 0)
    consts = [g, w_in, g_q, g_kv, w_uq_t, w_uk, w_uv_t, g_v]
    return pl.pallas_call(
        _even_in_body,
        grid=(t // tm,),
        in_specs=[pl.BlockSpec((tm, d), tok)] + [_resident(c.shape) for c in consts] + [
            pl.BlockSpec((tm, 2 * QK_ROPE), lambda i: (i % tiles_per_seq, 0)),
            pl.BlockSpec((2 * QK_ROPE, tm), lambda i: (0, i % tiles_per_seq)),
        ],
        out_specs=[
            pl.BlockSpec((MLA_HEADS, None, MLA_QK, tm), lambda i: (0, i // q_split, 0, i % q_split)),
            pl.BlockSpec((MLA_HEADS, tm, MLA_QK), lambda i: (0, i, 0)),
            pl.BlockSpec((MLA_HEADS, tm // MLA_KV_CHUNK, MLA_V_ROWS, MLA_KV_CHUNK), lambda i: (0, i, 0, 0)),
            pl.BlockSpec((tm, GM_WIDTH), tok),
            pl.BlockSpec((tm, GM_WIDTH), tok),
        ],
        out_shape=[
            jax.ShapeDtypeStruct((MLA_HEADS, t // MLA_TQ, MLA_QK, MLA_TQ), BF16),
            jax.ShapeDtypeStruct((MLA_HEADS, t, MLA_QK), BF16),
            jax.ShapeDtypeStruct((MLA_HEADS, t // MLA_KV_CHUNK, MLA_V_ROWS, MLA_KV_CHUNK), BF16),
            jax.ShapeDtypeStruct((t, GM_WIDTH), BF16),
            jax.ShapeDtypeStruct((t, GM_WIDTH), BF16),
        ],
        compiler_params=_params("parallel"),
        name="even_in",
    )(x, *consts, tab, tab.T)


def _mla_body(qt_ref, k_ref, vt_ref, o_ref, acc_ref, s_ref, *, tk):
    n_tiles, _, tq = qt_ref.shape
    sub = tk // MLA_KV_CHUNK
    n_chunks = k_ref.shape[0] // tk
    chunk_bits = n_chunks.bit_length() - 1
    assert n_chunks == 1 << chunk_bits and n_chunks % MLA_SLOTS == 0
    last = n_tiles * n_chunks - 1

    def scores(v, slot):
        tile = lax.shift_right_logical(v, jnp.int32(chunk_bits))
        off = pl.multiple_of((v & (n_chunks - 1)) * tk, tk)
        st = jnp.dot(k_ref[pl.ds(off, tk), :], qt_ref[tile], preferred_element_type=F32)
        s_ref[slot] = st
        return jnp.max(st, axis=0, keepdims=True)

    def absorb(c, slot, m_chunk, m_old):
        m_new = jnp.maximum(m_old, m_chunk)
        alpha = jnp.exp2(m_old - m_new)
        pb = jnp.exp2(s_ref[slot] - m_new).astype(BF16)
        vt = jnp.concatenate([vt_ref[c * sub + j] for j in range(sub)], axis=1)
        pv = jnp.dot(vt, pb, preferred_element_type=F32)
        acc_ref[...] = alpha * acc_ref[...] + pv
        return m_new

    def tile_step(tile, m_chunk):
        acc_ref[...] = jnp.zeros(acc_ref.shape, F32)
        m = jnp.full((1, tq), -jnp.inf, F32)
        for c in range(n_chunks):
            nxt = tile * n_chunks + c + 1
            if c == n_chunks - 1:
                nxt = jnp.minimum(nxt, last)
            m_next = scores(nxt, (c + 1) % MLA_SLOTS)
            m = absorb(c, c % MLA_SLOTS, m_chunk, m)
            m_chunk = m_next
        out = acc_ref[:V_HEAD, :] / acc_ref[V_HEAD:V_HEAD + 1, :]
        o_ref[pl.ds(pl.multiple_of(tile * tq, tq), tq), :] = out.T.astype(o_ref.dtype)
        return m_chunk

    lax.fori_loop(0, n_tiles, tile_step, scores(jnp.int32(0), 0))


def _mla(qt, k, vt, n_seq, seq, *, tk=512):
    t = n_seq * seq
    span = MLA_Q_TILES * MLA_TQ
    nq = seq // span
    assert seq % span == 0 and seq % tk == 0
    return pl.pallas_call(
        functools.partial(_mla_body, tk=tk),
        grid=(n_seq, MLA_HEADS, nq),
        in_specs=[
            pl.BlockSpec((None, MLA_Q_TILES, MLA_QK, MLA_TQ), lambda b, h, i: (h, b * nq + i, 0, 0)),
            pl.BlockSpec((None, seq, MLA_QK), lambda b, h, i: (h, b, 0)),
            pl.BlockSpec((None, seq // MLA_KV_CHUNK, MLA_V_ROWS, MLA_KV_CHUNK), lambda b, h, i: (h, b, 0, 0)),
        ],
        out_specs=pl.BlockSpec((span, V_HEAD), lambda b, h, i: (b * nq + i, h)),
        out_shape=jax.ShapeDtypeStruct((t, MLA_HEADS * V_HEAD), BF16),
        scratch_shapes=[pltpu.VMEM((MLA_V_ROWS, MLA_TQ), F32), pltpu.VMEM((MLA_SLOTS, tk, MLA_TQ), F32)],
        compiler_params=_params("parallel", "parallel", "arbitrary"),
        name="mla_attention",
    )(qt, k, vt)


def _even_out_body(x_ref, a_ref, u_ref, vl_ref, ws_ref, bs_ref, wo_ref, g_ref, o_ref, b_ref):
    tm = x_ref.shape[0]
    for c in range(tm // GM_CHUNK):
        rows = slice(c * GM_CHUNK, (c + 1) * GM_CHUNK)
        for grp in range(GM_GROUPS):
            cols = slice(grp * 128, (grp + 1) * 128)
            mixed = jnp.dot(ws_ref[grp], vl_ref[rows, cols], preferred_element_type=F32)
            mixed = mixed + bs_ref[:, grp:grp + 1]
            b_ref[rows, cols] = (u_ref[rows, cols].astype(F32) * mixed).astype(BF16)
    n_a = a_ref.shape[1]
    y = jnp.dot(a_ref[...], wo_ref[:n_a, :], preferred_element_type=F32)
    y = y + jnp.dot(b_ref[...], wo_ref[n_a:, :], preferred_element_type=F32)
    o_ref[...] = x_ref[...] + _rms(y, g_ref[...])


def _even_out(x, a, u, vl, w_s, b_s_t, w_o, g, *, tm=512):
    t, d = x.shape
    tok = lambda i: (i, 0)
    return pl.pallas_call(
        _even_out_body,
        grid=(t // tm,),
        in_specs=[
            pl.BlockSpec((tm, d), tok),
            pl.BlockSpec((tm, a.shape[1]), tok),
            pl.BlockSpec((tm, GM_WIDTH), tok),
            pl.BlockSpec((tm, GM_WIDTH), tok),
            _resident(w_s.shape), _resident(b_s_t.shape), _resident(w_o.shape), _resident(g.shape),
        ],
        out_specs=pl.BlockSpec((tm, d), tok),
        out_shape=jax.ShapeDtypeStruct((t, d), F32),
        scratch_shapes=[pltpu.VMEM((tm, GM_WIDTH), BF16)],
        compiler_params=_params("parallel"),
        name="even_out",
    )(x, a, u, vl, w_s, b_s_t, w_o, g)


def _odd_in_body(x_ref, g_ref, wqt_ref, wk_ref, wvt_ref, qt_ref, k_ref, vt_ref):
    h = _rms(x_ref[...], g_ref[...]).astype(BF16)
    qt = lax.dot_general(wqt_ref[...], h, _NT, preferred_element_type=F32)
    qt_ref[...] = (qt * (HEAD_DIM ** -0.5 * LOG2E)).astype(BF16)
    k_ref[...] = jnp.dot(h, wk_ref[...], preferred_element_type=F32).astype(BF16)
    vt_ref[...] = lax.dot_general(wvt_ref[...], h, _NT, preferred_element_type=F32).astype(BF16)


def _odd_in(x, g, w_q_t, w_k, w_v_t, *, tm=512):
    t, d = x.shape
    nq, nk = w_q_t.shape[0], w_k.shape[1]
    consts = [g, w_q_t, w_k, w_v_t]
    return pl.pallas_call(
        _odd_in_body,
        grid=(t // tm,),
        in_specs=[pl.BlockSpec((tm, d), lambda i: (i, 0))] + [_resident(c.shape) for c in consts],
        out_specs=[pl.BlockSpec((nq, tm), lambda i: (0, i)), pl.BlockSpec((tm, nk), lambda i: (i, 0)),
                   pl.BlockSpec((nk, tm), lambda i: (0, i))],
        out_shape=[jax.ShapeDtypeStruct((nq, t), BF16), jax.ShapeDtypeStruct((t, nk), BF16),
                   jax.ShapeDtypeStruct((nk, t), BF16)],
        compiler_params=_params("parallel"),
        name="odd_in",
    )(x, *consts)


def _window_body(sink_ref, x_ref, qt_ref, kp_ref, km_ref, kn_ref, vtp_ref, vtm_ref, vtn_ref, wo_ref, g_ref,
                 o_ref, k_scr, vt_scr, a_scr, a_prev, y_scr, *, seq, n_tiles):
    blk = WINDOW
    tq = qt_ref.shape[1]
    step = pl.program_id(0)

    @pl.when(step == 0)
    def _():
        a_scr[...] = jnp.zeros(a_scr.shape, a_scr.dtype)

    a_prev[...] = a_scr[...]
    d_model = wo_ref.shape[1]
    n_chains = (tq // blk) * GQA_KV_HEADS
    piece = d_model // (n_chains // 2)

    k_scr[:blk] = kp_ref[...]
    k_scr[blk:blk + tq] = km_ref[...]
    k_scr[blk + tq:] = kn_ref[...]
    vt_scr[:, :blk] = vtp_ref[...]
    vt_scr[:, blk:blk + tq] = vtm_ref[...]
    vt_scr[:, blk + tq:] = vtn_ref[...]

    j_idx = lax.broadcasted_iota(jnp.int32, (3 * blk, blk), 0)
    a_idx = lax.broadcasted_iota(jnp.int32, (3 * blk, blk), 1)
    dist = jnp.abs(j_idx - blk - a_idx)
    tile_start = (jnp.minimum(step, n_tiles - 1) % (seq // tq)) * tq

    for r in range(tq // blk):
        key_pos = tile_start + (r - 1) * blk + j_idx
        valid = (dist <= WINDOW) & (key_pos >= 0) & (key_pos < seq)
        neg_dist = jnp.where(valid, dist.astype(F32) * -LOG2E, -jnp.inf)
        q_lanes = slice(r * blk, (r + 1) * blk)
        win = slice(r * blk, (r + 3) * blk)
        for kvh in range(GQA_KV_HEADS):
            chain = r * GQA_KV_HEADS + kvh
            if chain % 2 == 0:
                cols = slice((chain // 2) * piece, (chain // 2 + 1) * piece)
                y_scr[:, cols] = jnp.dot(a_prev[...], wo_ref[:, cols], preferred_element_type=F32)
            feat = slice(kvh * HEAD_DIM, (kvh + 1) * HEAD_DIM)
            heads = [kvh * GQA_GROUP + grp for grp in range(GQA_GROUP)]
            qt = jnp.concatenate([qt_ref[hd * HEAD_DIM:(hd + 1) * HEAD_DIM, q_lanes] for hd in heads], axis=1)
            st = jnp.dot(k_scr[win, feat], qt, preferred_element_type=F32)
            bias = jnp.concatenate([2.0 ** (-8.0 * (hd + 1) / GQA_HEADS) * neg_dist for hd in heads], axis=1)
            sink = jnp.concatenate([jnp.full((1, blk), sink_ref[hd] * LOG2E, F32) for hd in heads], axis=1)
            logits = st + bias
            m = jnp.maximum(jnp.max(logits, axis=0, keepdims=True), sink)
            e = jnp.exp2(logits - m)
            denom = jnp.sum(e, axis=0, keepdims=True) + jnp.exp2(sink - m)
            ot = jnp.dot(vt_scr[feat, win], e.astype(BF16), preferred_element_type=F32) / denom
            for grp, hd in enumerate(heads):
                a_scr[q_lanes, hd * HEAD_DIM:(hd + 1) * HEAD_DIM] = (
                    ot[:, grp * blk:(grp + 1) * blk].T.astype(a_scr.dtype))
    o_ref[...] = x_ref[...] + _rms(y_scr[...], g_ref[...])


def _window(x, qt, k, vt, sink, w_o, g, n_seq, seq, *, tq=512):
    t, d = x.shape
    nq = seq // tq
    n_tiles = n_seq * nq
    r = tq // WINDOW
    nk = k.shape[1]
    blocks_per_seq = seq // WINDOW
    main = lambda s: jnp.minimum(s, n_tiles - 1)
    lagged = lambda s: jnp.maximum(s - 1, 0)
    first = lambda s: (main(s) // nq) * blocks_per_seq
    prev = lambda s: first(s) + jnp.maximum((main(s) % nq) * r - 1, 0)
    nxt = lambda s: first(s) + jnp.minimum((main(s) % nq + 1) * r, blocks_per_seq - 1)
    rows = lambda size, idx: pl.BlockSpec((size, nk), lambda s: (idx(s), 0))
    lanes = lambda size, idx: pl.BlockSpec((nk, size), lambda s: (0, idx(s)))
    tok = pl.BlockSpec((tq, d), lambda s: (lagged(s), 0))
    return pl.pallas_call(
        functools.partial(_window_body, seq=seq, n_tiles=n_tiles),
        grid=(n_tiles + 1,),
        in_specs=[pl.BlockSpec(memory_space=pltpu.SMEM), tok,
                  pl.BlockSpec((qt.shape[0], tq), lambda s: (0, main(s))),
                  rows(WINDOW, prev), rows(tq, main), rows(WINDOW, nxt),
                  lanes(WINDOW, prev), lanes(tq, main), lanes(WINDOW, nxt),
                  _resident(w_o.shape), _resident(g.shape)],
        out_specs=tok,
        out_shape=jax.ShapeDtypeStruct((t, d), F32),
        scratch_shapes=[pltpu.VMEM((tq + 2 * WINDOW, nk), BF16), pltpu.VMEM((nk, tq + 2 * WINDOW), BF16),
                        pltpu.VMEM((tq, qt.shape[0]), BF16), pltpu.VMEM((tq, qt.shape[0]), BF16),
                        pltpu.VMEM((tq, d), F32)],
        compiler_params=_params("arbitrary"),
        name="window_attention",
    )(sink, x, qt, k, k, k, vt, vt, vt, w_o, g)


def _swap_halves(w):
    half = w.shape[-1] // 2
    return jnp.concatenate([-w[..., half:], w[..., :half]], axis=-1)


def _prep_even(w_in, w_uq, w_ukv):
    o_u = Q_LORA + KV_LORA
    o_g = o_u + QK_ROPE
    k_r = w_in[:, o_u:o_g]
    w_in_x = jnp.concatenate([w_in[:, :o_u], w_in[:, o_g:], k_r, _swap_halves(k_r)], axis=-1)
    per_head = w_uq.reshape(w_uq.shape[0], MLA_HEADS, QK_NOPE + QK_ROPE)
    rope = per_head[..., QK_NOPE:]
    w_uq_x = jnp.concatenate([per_head, _swap_halves(rope)], axis=-1).reshape(w_uq.shape[0], MLA_HEADS * MLA_QK)
    kv = w_ukv.reshape(w_ukv.shape[0], MLA_HEADS, QK_NOPE + V_HEAD)
    w_uk = kv[..., :QK_NOPE].reshape(w_ukv.shape[0], MLA_HEADS * QK_NOPE)
    w_uv = kv[..., QK_NOPE:].reshape(w_ukv.shape[0], MLA_HEADS * V_HEAD)
    return w_in_x.astype(BF16), w_uq_x.T.astype(BF16), w_uk.astype(BF16), w_uv.T.astype(BF16)


def _prep_odd(w_qkv):
    nq = GQA_HEADS * HEAD_DIM
    nk = GQA_KV_HEADS * HEAD_DIM
    return (w_qkv[:, :nq].T.astype(BF16), w_qkv[:, nq:nq + nk].astype(BF16), w_qkv[:, nq + nk:].T.astype(BF16))


def _rope_table(seq):
    pos = jnp.arange(seq, dtype=F32)
    inv = ROPE_THETA ** (-jnp.arange(0, QK_ROPE, 2, dtype=F32) / QK_ROPE)
    ang = pos[:, None] * inv[None, :]
    cos, sin = jnp.cos(ang), jnp.sin(ang)
    return jnp.concatenate([cos, cos, sin, sin], axis=-1)


def _trunk(x3, w):
    n_seq, seq, d = x3.shape
    x = x3.reshape(n_seq * seq, d)
    depth = w["norm_g"].shape[0]
    for layer in range(depth):
        g = w["norm_g"][layer]
        row = lambda i: g[i][None, :]
        ffn = lambda x, n, idx: _ffn(x, row(n), row(n + 1), w["gate"], w["up"], w["down"], layer, idx)
        x = ffn(x, 0, 0)
        i = layer // 2
        if layer % 2 == 0:
            w_in, w_uq_t, w_uk, w_uv_t = w["even"][i]
            qt, k, vt, u, vl = _even_in(x, row(2), w_in, w["g_q"][i][None, :], w["g_kv"][i][None, :],
                                        w_uq_t, w_uk, w_uv_t, w["g_v"][i][None, :], _rope_table(seq), seq)
            a = _mla(qt, k, vt, n_seq, seq)
            x = _even_out(x, a, u, vl, w["w_s"][i], w["b_s_t"][i], w["even_o"][i], row(3))
        else:
            qt, k, vt = _odd_in(x, row(2), *w["odd"][i])
            x = _window(x, qt, k, vt, w["sink"][i], w["odd_o"][i], row(3), n_seq, seq)
        x = ffn(x, 4, 1)
    return x.reshape(n_seq, seq, d)


def kernel(x_prompt, x_sample, norm_g, ffn_w_gate, ffn_w_up, ffn_w_down, even_w_in, mla_g_q, mla_g_kv, mla_w_uq, mla_w_ukv, gm_g_v, gm_w_s, gm_b_s, even_w_o, odd_w_qkv, odd_sink, odd_w_o):
    even = [_prep_even(even_w_in[i], mla_w_uq[i], mla_w_ukv[i]) for i in range(even_w_in.shape[0])]
    w = {
        "norm_g": norm_g,
        "gate": ffn_w_gate.astype(BF16), "up": ffn_w_up.astype(BF16), "down": ffn_w_down.astype(BF16),
        "even": even, "g_q": mla_g_q, "g_kv": mla_g_kv, "g_v": gm_g_v,
        "w_s": gm_w_s.astype(BF16), "b_s_t": jnp.swapaxes(gm_b_s, 1, 2), "even_o": even_w_o.astype(BF16),
        "odd": [_prep_odd(odd_w_qkv[i]) for i in range(odd_w_qkv.shape[0])],
        "sink": odd_sink, "odd_o": odd_w_o.astype(BF16),
    }
    return (_trunk(x_prompt, w), _trunk(x_sample, w))
```
